```python
import math
import jax, jax.numpy as jnp
from jax import lax
import numpy as np

D_MODEL = 1024
BATCH = 2
SEQ = 8192
DEPTH = 4
DEC_BATCH = 128
DEC_SEQ = 8
PAST_LEN = 8192
PAGE_SIZE = 128

HEAD_DIM = 64
BLK = 128
DILATED = ((128, 1), (512, 4), (2048, 16))
N_DIL = len(DILATED)
H_A = D_MODEL // (2 * HEAD_DIM)
D_A = H_A * HEAD_DIM
QKV_A = N_DIL * 3 * D_A
D_B = D_MODEL // 2
GROUP_CH = 16
G_B = D_B // GROUP_CH
N_STATE = 64
H_C = D_MODEL // HEAD_DIM
KV_C = H_C // 4
WIN_C = 128
D_FF = 4 * D_MODEL
D_PLE = 256
N_AB = (DEPTH + 1) // 2
N_C = DEPTH // 2
RMS_EPS = 1e-6
NEG_INF = -1e30

kernel_name = 'hybrid_dilated_s5_sinkswa_decoder_step'


def rmsnorm(x, g):
    x32 = x.astype(jnp.float32)
    y = x32 * lax.rsqrt(jnp.mean(x32 * x32, axis=-1, keepdims=True) + RMS_EPS)
    return (y * g.astype(jnp.float32)).astype(x.dtype)


def _masked_softmax_parts(s, valid):
    s = jnp.where(valid, s, NEG_INF)
    m = s.max(-1)
    pexp = jnp.exp(s - m[..., None])
    return pexp, m, pexp.sum(-1)


def banded_attn(q, k, v, n_steps):
    n, t, h, hd = q.shape
    kv = k.shape[2]
    rep = h // kv
    nb = -(-t // BLK)
    tp = nb * BLK
    pad = tp - t
    qb = jnp.pad(q, ((0, 0), (0, pad), (0, 0), (0, 0))).reshape(n, nb, BLK, kv, rep, hd)

    def key_blocks(a):
        ap = jnp.pad(a, ((0, 0), (BLK, pad), (0, 0), (0, 0)))
        prev = ap[:, :tp].reshape(n, nb, BLK, kv, hd)
        cur = ap[:, BLK:].reshape(n, nb, BLK, kv, hd)
        return jnp.concatenate([prev, cur], axis=2)

    kb, vb = key_blocks(k), key_blocks(v)
    s = jnp.einsum('nbqgrd,nbkgd->nbgrqk', qb, kb, preferred_element_type=jnp.float32) * (hd ** -0.5)
    qi = jnp.arange(BLK)[:, None]
    ki = jnp.arange(2 * BLK)[None, :]
    dist = qi + BLK - ki
    kabs = jnp.arange(nb)[:, None, None] * BLK - BLK + ki[None]
    valid = (dist >= 0) & (dist <= n_steps) & (kabs >= 0)
    pexp, m, l = _masked_softmax_parts(s, valid[None, :, None, None])
    o = jnp.einsum('nbgrqk,nbkgd->nbqgrd', pexp, vb.astype(jnp.float32))
    o = o.reshape(n, tp, h, hd)[:, :t]
    m = m.transpose(0, 1, 4, 2, 3).reshape(n, tp, h)[:, :t]
    l = l.transpose(0, 1, 4, 2, 3).reshape(n, tp, h)[:, :t]
    return o, m, l


def gathered_attn(q, kseq, vseq, n_past, dilation, n_steps):
    n, s_new, h, hd = q.shape
    kv = kseq.shape[2]
    rep = h // kv
    idx = n_past + jnp.arange(s_new)[:, None] - dilation * jnp.arange(n_steps + 1)[None, :]
    valid = idx >= 0
    idx = jnp.maximum(idx, 0)
    kg = kseq[:, idx]
    vg = vseq[:, idx]
    qg = q.reshape(n, s_new, kv, rep, hd)
    sc = jnp.einsum('nsgrd,nskgd->nsgrk', qg, kg, preferred_element_type=jnp.float32) * (hd ** -0.5)
    pexp, m, l = _masked_softmax_parts(sc, valid[None, :, None, None, :])
    o = jnp.einsum('nsgrk,nskgd->nsgrd', pexp, vg.astype(jnp.float32))
    return o.reshape(n, s_new, h, hd), m.reshape(n, s_new, h), l.reshape(n, s_new, h)


def dilated_prompt(q, k, v, dilation, n_steps):
    n, t, h, hd = q.shape
    ts = t // dilation

    def split(a):
        return a.reshape(n, ts, dilation, a.shape[2], hd).transpose(0, 2, 1, 3, 4).reshape(n * dilation, ts, a.shape[2], hd)

    o, m, l = banded_attn(split(q), split(k), split(v), n_steps)
    o = o.reshape(n, dilation, ts, h, hd).transpose(0, 2, 1, 3, 4).reshape(n, t, h, hd)
    m = m.reshape(n, dilation, ts, h).transpose(0, 2, 1, 3).reshape(n, t, h)
    l = l.reshape(n, dilation, ts, h).transpose(0, 2, 1, 3).reshape(n, t, h)
    return o, m, l


def s5_scan(u, a_re, a_im, log_dt, b_re, b_im, c_re, c_im, d_skip, h0):
    f32 = jnp.float32
    n, t, _ = u.shape
    lam = lax.complex(a_re.astype(f32), a_im.astype(f32))
    dt = jnp.exp(log_dt.astype(f32))[:, None]
    abar = jnp.exp(lam * dt)
    bbar = ((abar - 1.0) / lam)[..., None] * lax.complex(b_re.astype(f32), b_im.astype(f32))
    u32 = u.astype(f32)
    ug = u32.reshape(n, t, G_B, GROUP_CH).astype(jnp.complex64)
    bu = jnp.einsum('ntgc,gsc->ntgs', ug, bbar)
    if h0 is not None:
        bu = bu.at[:, 0].add(abar * h0)
    a = jnp.broadcast_to(abar, bu.shape)
    _, hs = lax.associative_scan(lambda e1, e2: (e1[0] * e2[0], e2[0] * e1[1] + e2[1]), (a, bu), axis=1)
    cmat = lax.complex(c_re.astype(f32), c_im.astype(f32))
    y = jnp.einsum('ntgs,gcs->ntgc', hs, cmat).real.reshape(n, t, D_B) + d_skip.astype(f32) * u32
    return y, hs[:, -1]


def mixer_ab(h, prm, j, kv_bufs, h0):
    n, t, _ = h.shape
    z = h @ prm['w_in_ab'][j]
    qkv = z[..., :QKV_A].reshape(n, t, N_DIL, 3, H_A, HEAD_DIM)
    u = z[..., QKV_A:]
    outs, ms, ls, new_bufs = [], [], [], []
    for g, (win, dil) in enumerate(DILATED):
        q, k, v = qkv[:, :, g, 0], qkv[:, :, g, 1], qkv[:, :, g, 2]
        if kv_bufs is None:
            o, m, l = dilated_prompt(q, k, v, dil, win // dil)
            keep = min(win, t)
            new_bufs.append(jnp.stack([k[:, t - keep:], v[:, t - keep:]], axis=1))
        else:
            buf = kv_bufs[g]
            n_past = buf.shape[2]
            kseq = jnp.concatenate([buf[:, 0].astype(k.dtype), k], axis=1)
            vseq = jnp.concatenate([buf[:, 1].astype(v.dtype), v], axis=1)
            o, m, l = gathered_attn(q, kseq, vseq, n_past, dil, win // dil)
            new_bufs.append(jnp.stack([kseq[:, -n_past:], vseq[:, -n_past:]], axis=1))
        outs.append(o)
        ms.append(m)
        ls.append(l)
    m_all = jnp.stack(ms)
    wts = jnp.exp(m_all - m_all.max(0))
    num = jnp.sum(wts[..., None] * jnp.stack(outs), axis=0)
    den = jnp.sum(wts * jnp.stack(ls), axis=0)
    out_a = (num / den[..., None]).reshape(n, t, D_A)
    y, h_last = s5_scan(u, prm['ssm_a_re'][j], prm['ssm_a_im'][j], prm['ssm_log_dt'][j], prm['ssm_b_re'][j],
                        prm['ssm_b_im'][j], prm['ssm_c_re'][j], prm['ssm_c_im'][j], prm['ssm_d'][j], h0)
    zb = jax.nn.gelu(y)
    out_b = zb * jax.nn.sigmoid(zb @ prm['w_glu'][j].astype(jnp.float32) + prm['b_glu'][j].astype(jnp.float32))
    mix = jnp.concatenate([out_a, out_b], axis=-1).astype(h.dtype) @ prm['w_out_ab'][j]
    return mix, new_bufs, h_last.real, h_last.imag


def mixer_c(h, w_in, w_out, sinks, kv_buf):
    n, t, _ = h.shape
    z = h @ w_in
    q = z[..., :H_C * HEAD_DIM].reshape(n, t, H_C, HEAD_DIM)
    k = z[..., H_C * HEAD_DIM:(H_C + KV_C) * HEAD_DIM].reshape(n, t, KV_C, HEAD_DIM)
    v = z[..., (H_C + KV_C) * HEAD_DIM:].reshape(n, t, KV_C, HEAD_DIM)
    if kv_buf is None:
        o, m, l = banded_attn(q, k, v, WIN_C)
        keep = min(WIN_C, t)
        new_buf = jnp.stack([k[:, t - keep:], v[:, t - keep:]], axis=1)
    else:
        n_past = kv_buf.shape[2]
        kseq = jnp.concatenate([kv_buf[:, 0].astype(k.dtype), k], axis=1)
        vseq = jnp.concatenate([kv_buf[:, 1].astype(v.dtype), v], axis=1)
        o, m, l = gathered_attn(q, kseq, vseq, n_past, 1, WIN_C)
        new_buf = jnp.stack([kseq[:, -n_past:], vseq[:, -n_past:]], axis=1)
    sk = sinks.astype(jnp.float32)
    mx = jnp.maximum(m, sk)
    sc = jnp.exp(m - mx)
    out = o * sc[..., None] / (l * sc + jnp.exp(sk - mx))[..., None]
    return out.reshape(n, t, H_C * HEAD_DIM).astype(h.dtype) @ w_out, new_buf


def trunk(x, p, caches, prm):
    new_a = [[] for _ in range(N_DIL)]
    new_re, new_im, new_c = [], [], []
    for i in range(DEPTH):
        j = i // 2
        h = rmsnorm(x, prm['g_pre_mix'][i])
        if i % 2 == 0:
            bufs = None if caches is None else [caches[g][j] for g in range(N_DIL)]
            h0 = None if caches is None else lax.complex(caches[N_DIL][j].astype(jnp.float32), caches[N_DIL + 1][j].astype(jnp.float32))
            mix, bufs_new, s_re, s_im = mixer_ab(h, prm, j, bufs, h0)
            for g in range(N_DIL):
                new_a[g].append(bufs_new[g])
            new_re.append(s_re)
            new_im.append(s_im)
        else:
            mix, c_new = mixer_c(h, prm['w_in_c'][j], prm['w_out_c'][j], prm['sinks_c'][j],
                                 None if caches is None else caches[N_DIL + 2][j])
            new_c.append(c_new)
        x = x + rmsnorm(mix, prm['g_post_mix'][i])
        hf = rmsnorm(x, prm['g_pre_ffn'][i])
        f = jnp.square(jax.nn.relu(hf @ prm['w_ff1'][i])) @ prm['w_ff2'][i]
        x = x + rmsnorm(f, prm['g_post_ffn'][i])
        gate = jax.nn.sigmoid(rmsnorm(x, prm['g_ple'][i]) @ prm['w_ple_gate'][i])
        x = x + (p[i] @ prm['w_ple'][i]) * gate
    return (x, jnp.stack(new_a[0]), jnp.stack(new_a[1]), jnp.stack(new_a[2]),
            jnp.stack(new_re), jnp.stack(new_im), jnp.stack(new_c))


def setup_inputs(seed: int = 0) -> dict:
    key = jax.random.key(seed)
    ks = iter(jax.random.split(key, 40))

    def nrm(shape, scale=1.0):
        return scale * jax.random.normal(next(ks), shape, jnp.float32)

    la = [min(w, PAST_LEN) for w, _ in DILATED]
    lc = min(WIN_C, PAST_LEN)
    d_in_c = (H_C + 2 * KV_C) * HEAD_DIM
    return {
        'x_prompt': nrm((BATCH, SEQ, D_MODEL)),
        'x_sample': nrm((DEC_BATCH, DEC_SEQ, D_MODEL)),
        'cache_a1_kv': nrm((N_AB, DEC_BATCH, 2, la[0], H_A, HEAD_DIM)),
        'cache_a2_kv': nrm((N_AB, DEC_BATCH, 2, la[1], H_A, HEAD_DIM)),
        'cache_a3_kv': nrm((N_AB, DEC_BATCH, 2, la[2], H_A, HEAD_DIM)),
        'state_b_re': nrm((N_AB, DEC_BATCH, G_B, N_STATE), 0.3),
        'state_b_im': nrm((N_AB, DEC_BATCH, G_B, N_STATE), 0.3),
        'cache_c_kv': nrm((N_C, DEC_BATCH, 2, lc, KV_C, HEAD_DIM)),
        'p_prompt': nrm((DEPTH, BATCH, SEQ, D_PLE)),
        'p_sample': nrm((DEPTH, DEC_BATCH, DEC_SEQ, D_PLE)),
        'w_in_ab': nrm((N_AB, D_MODEL, QKV_A + D_B), D_MODEL ** -0.5),
        'w_out_ab': nrm((N_AB, D_A + D_B, D_MODEL), (D_A + D_B) ** -0.5),
        'ssm_a_re': -0.5 + nrm((N_AB, G_B, N_STATE), 0.01),
        'ssm_a_im': math.pi * jnp.arange(N_STATE, dtype=jnp.float32) + nrm((N_AB, G_B, N_STATE), 0.01),
        'ssm_log_dt': jax.random.uniform(next(ks), (N_AB, G_B), jnp.float32, math.log(1e-3), math.log(1e-1)),
        'ssm_b_re': nrm((N_AB, G_B, N_STATE, GROUP_CH), (2 * GROUP_CH) ** -0.5),
        'ssm_b_im': nrm((N_AB, G_B, N_STATE, GROUP_CH), (2 * GROUP_CH) ** -0.5),
        'ssm_c_re': nrm((N_AB, G_B, GROUP_CH, N_STATE), N_STATE ** -0.5),
        'ssm_c_im': nrm((N_AB, G_B, GROUP_CH, N_STATE), N_STATE ** -0.5),
        'ssm_d': nrm((N_AB, D_B), 0.5),
        'w_glu': nrm((N_AB, D_B, D_B), D_B ** -0.5),
        'b_glu': nrm((N_AB, D_B), 0.01),
        'w_in_c': nrm((N_C, D_MODEL, d_in_c), D_MODEL ** -0.5),
        'sinks_c': nrm((N_C, H_C), 0.5),
        'w_out_c': nrm((N_C, H_C * HEAD_DIM, D_MODEL), (H_C * HEAD_DIM) ** -0.5),
        'g_pre_mix': 1.0 + nrm((DEPTH, D_MODEL), 0.01),
        'g_post_mix': 1.0 + nrm((DEPTH, D_MODEL), 0.01),
        'g_pre_ffn': 1.0 + nrm((DEPTH, D_MODEL), 0.01),
        'g_post_ffn': 1.0 + nrm((DEPTH, D_MODEL), 0.01),
        'g_ple': 1.0 + nrm((DEPTH, D_MODEL), 0.01),
        'w_ff1': nrm((DEPTH, D_MODEL, D_FF), D_MODEL ** -0.5),
        'w_ff2': nrm((DEPTH, D_FF, D_MODEL), D_FF ** -0.5),
        'w_ple': nrm((DEPTH, D_PLE, D_MODEL), D_PLE ** -0.5),
        'w_ple_gate': nrm((DEPTH, D_MODEL, D_MODEL), D_MODEL ** -0.5),
    }


def reference(x_prompt, x_sample, cache_a1_kv, cache_a2_kv, cache_a3_kv, state_b_re, state_b_im, cache_c_kv,
              p_prompt, p_sample, w_in_ab, w_out_ab, ssm_a_re, ssm_a_im, ssm_log_dt, ssm_b_re, ssm_b_im,
              ssm_c_re, ssm_c_im, ssm_d, w_glu, b_glu, w_in_c, sinks_c, w_out_c, g_pre_mix, g_post_mix,
              g_pre_ffn, g_post_ffn, g_ple, w_ff1, w_ff2, w_ple, w_ple_gate):
    prm = dict(w_in_ab=w_in_ab, w_out_ab=w_out_ab, ssm_a_re=ssm_a_re, ssm_a_im=ssm_a_im, ssm_log_dt=ssm_log_dt,
               ssm_b_re=ssm_b_re, ssm_b_im=ssm_b_im, ssm_c_re=ssm_c_re, ssm_c_im=ssm_c_im, ssm_d=ssm_d,
               w_glu=w_glu, b_glu=b_glu, w_in_c=w_in_c, sinks_c=sinks_c, w_out_c=w_out_c,
               g_pre_mix=g_pre_mix, g_post_mix=g_post_mix, g_pre_ffn=g_pre_ffn, g_post_ffn=g_post_ffn,
               g_ple=g_ple, w_ff1=w_ff1, w_ff2=w_ff2, w_ple=w_ple, w_ple_gate=w_ple_gate)
    y_prompt, a1_p, a2_p, a3_p, bre_p, bim_p, c_p = trunk(x_prompt, p_prompt, None, prm)
    caches = (cache_a1_kv, cache_a2_kv, cache_a3_kv, state_b_re, state_b_im, cache_c_kv)
    y_sample, a1_s, a2_s, a3_s, bre_s, bim_s, c_s = trunk(x_sample, p_sample, caches, prm)
    return (y_prompt, y_sample, a1_p, a1_s, a2_p, a2_s, a3_p, a3_s, bre_p, bre_s, bim_p, bim_s, c_p, c_s)
```

```python
import functools

import jax
import jax.numpy as jnp
from jax import lax
from jax.experimental import pallas as pl
from jax.experimental.pallas import tpu as pltpu

F32 = jnp.float32
BF16 = jnp.bfloat16

HEAD_DIM = 64
BLK = 128
DILATED = ((128, 1), (512, 4), (2048, 16))
GROUP_CH = 16
N_STATE = 64
WIN_C = 128
RMS_EPS = 1e-6
NEG_INF = -1e30
SUBLANES = 8
LANES = 128
VMEM_LIMIT = 48 * 1024 * 1024


def _pick(n, prefs):
    for p in prefs:
        if n % p == 0:
            return p
    raise ValueError(f"no tile in {prefs} divides {n}")


def _rms(x, g):
    ms = jnp.mean(x * x, axis=-1, keepdims=True)
    return x * lax.rsqrt(ms + RMS_EPS) * g


def _const_spec(shape):
    nd = len(shape)
    return pl.BlockSpec(shape, lambda *_: (0,) * nd)


def _norm_matmul_kernel(x_ref, g_ref, w_ref, o_ref, h_ref):
    @pl.when(pl.program_id(1) == 0)
    def _():
        h_ref[...] = _rms(x_ref[...], g_ref[...]).astype(BF16)

    o_ref[...] = jnp.dot(h_ref[...], w_ref[...], preferred_element_type=F32)


def norm_matmul(x, g, w):
    m, d = x.shape
    n = w.shape[1]
    tm = _pick(m, (1024, 512, 256, 128))
    tn = _pick(n, (512, 256, 128))
    return pl.pallas_call(
        _norm_matmul_kernel,
        grid=(m // tm, n // tn),
        in_specs=[pl.BlockSpec((tm, d), lambda i, j: (i, 0)),
                  pl.BlockSpec((1, d), lambda i, j: (0, 0)),
                  pl.BlockSpec((d, tn), lambda i, j: (0, j))],
        out_specs=pl.BlockSpec((tm, tn), lambda i, j: (i, j)),
        out_shape=jax.ShapeDtypeStruct((m, n), F32),
        scratch_shapes=[pltpu.VMEM((tm, d), BF16)],
        compiler_params=pltpu.CompilerParams(dimension_semantics=("parallel", "arbitrary"),
                                             vmem_limit_bytes=VMEM_LIMIT),
        name="norm_matmul",
    )(x, g, w)


def _merge3(o, m, l, on1, ls1, on2, ls2):
    ls0 = m + jnp.log(l)
    mx = jnp.maximum(jnp.maximum(ls0, ls1), ls2)
    w0 = jnp.exp(ls0 - mx)
    w1 = jnp.exp(ls1 - mx)
    w2 = jnp.exp(ls2 - mx)
    return (w0 * (o / l) + w1 * on1 + w2 * on2) / (w0 + w1 + w2)


def _with_sink(o, m, l, sk):
    mx = jnp.maximum(m, sk)
    sc = jnp.exp(m - mx)
    return o * sc / (l * sc + jnp.exp(sk - mx))


def _prompt_attn_kernel(*refs, hq, hkv, mode):
    q_ref, kp_ref, kc_ref, vp_ref, vc_ref = refs[:5]
    rest = refs[5:]
    if mode == "sink":
        sink_ref, out_ref = rest
    elif mode == "merge":
        on1_ref, ls1_ref, on2_ref, ls2_ref, out_ref = rest
    else:
        on_ref, ls_ref = rest

    b = pl.program_id(2)
    qi = lax.broadcasted_iota(jnp.int32, (BLK, 2 * BLK), 0)
    ki = lax.broadcasted_iota(jnp.int32, (BLK, 2 * BLK), 1)
    valid = (ki >= qi) & (ki <= qi + BLK) & ((ki >= BLK) | (b > 0))

    kv = {}
    for h in range(hq):
        g = h % hkv
        if g not in kv:
            sl = slice(g * HEAD_DIM, (g + 1) * HEAD_DIM)
            k2 = jnp.concatenate([kp_ref[:, sl], kc_ref[:, sl]], axis=0).astype(BF16)
            v2 = jnp.concatenate([vp_ref[:, sl], vc_ref[:, sl]], axis=0).astype(BF16)
            kv[g] = (k2, v2)
        k2, v2 = kv[g]
        hs = slice(h * HEAD_DIM, (h + 1) * HEAD_DIM)
        qh = q_ref[:, hs].astype(BF16)
        s = lax.dot_general(qh, k2, (((1,), (1,)), ((), ())), preferred_element_type=F32)
        s = jnp.where(valid, s * (HEAD_DIM ** -0.5), NEG_INF)
        m = jnp.max(s, axis=-1, keepdims=True)
        p = jnp.exp(s - m)
        l = jnp.sum(p, axis=-1, keepdims=True)
        o = jnp.dot(p.astype(BF16), v2, preferred_element_type=F32)
        if mode == "sink":
            out_ref[:, hs] = _with_sink(o, m, l, sink_ref[:, hs])
        elif mode == "merge":
            out_ref[:, hs] = _merge3(o, m, l, on1_ref[:, hs], ls1_ref[:, hs], on2_ref[:, hs], ls2_ref[:, hs])
        else:
            on_ref[:, hs] = o / l
            ls_ref[:, hs] = jnp.broadcast_to(m + jnp.log(l), (BLK, HEAD_DIM))


def prompt_attn(z, *, batch, seq, dil, qoff, koff, voff, hq, hkv, mode, extras=(), sink=None):
    m_rows, nz = z.shape
    wq, wk = hq * HEAD_DIM, hkv * HEAD_DIM
    nbk = seq // dil // BLK
    zv = z.reshape(m_rows // dil, dil * nz)
    cq, ck, cv = qoff // wq, koff // wk, voff // wk

    def row(n, b):
        return n * nbk + b

    in_specs = [
        pl.BlockSpec((BLK, wq), lambda n, r, b: (row(n, b), r * (nz // wq) + cq)),
        pl.BlockSpec((BLK, wk), lambda n, r, b: (row(n, jnp.maximum(b - 1, 0)), r * (nz // wk) + ck)),
        pl.BlockSpec((BLK, wk), lambda n, r, b: (row(n, b), r * (nz // wk) + ck)),
        pl.BlockSpec((BLK, wk), lambda n, r, b: (row(n, jnp.maximum(b - 1, 0)), r * (nz // wk) + cv)),
        pl.BlockSpec((BLK, wk), lambda n, r, b: (row(n, b), r * (nz // wk) + cv)),
    ]
    args = [zv, zv, zv, zv, zv]
    out_block = pl.BlockSpec((BLK, wq), lambda n, r, b: (row(n, b), r))
    out_one = jax.ShapeDtypeStruct((m_rows // dil, dil * wq), F32)
    if mode == "sink":
        in_specs.append(pl.BlockSpec((1, wq), lambda n, r, b: (0, 0)))
        args.append(sink)
        out_specs, out_shape = out_block, out_one
    elif mode == "merge":
        assert dil == 1
        for e in extras:
            in_specs.append(pl.BlockSpec((BLK, wq), lambda n, r, b: (row(n, b), 0)))
            args.append(e)
        out_specs, out_shape = out_block, out_one
    else:
        out_specs, out_shape = [out_block, out_block], [out_one, out_one]

    res = pl.pallas_call(
        functools.partial(_prompt_attn_kernel, hq=hq, hkv=hkv, mode=mode),
        grid=(batch, dil, nbk),
        in_specs=in_specs, out_specs=out_specs, out_shape=out_shape,
        compiler_params=pltpu.CompilerParams(dimension_semantics=("parallel", "parallel", "arbitrary"),
                                             vmem_limit_bytes=VMEM_LIMIT),
        name=f"prompt_attn_d{dil}_{mode}",
    )(*args)
    if mode in ("sink", "merge"):
        return res.reshape(m_rows, wq)
    return res[0].reshape(m_rows, wq), res[1].reshape(m_rows, wq)


def _sample_attn_kernel(*refs, nb_seq, win, dil, hkv, rep, mode, n_alias, n_new):
    wk = hkv * HEAD_DIM
    q_ref, kn_ref, vn_ref, cache_ref, seg_ref = refs[:5]
    pos = 5
    if mode == "sink":
        sink_ref = refs[pos]
        pos += 1
    elif mode == "merge":
        on1_ref, ls1_ref, on2_ref, ls2_ref = refs[pos:pos + 4]
        pos += 4
    pos += n_alias
    co_ref = refs[pos]
    pos += 1
    if mode == "lse":
        on_ref, ls_ref = refs[pos:pos + 2]
        pos += 2
    else:
        out_ref = refs[pos]
        pos += 1
    ks_ref = refs[pos]

    n_keys = BLK + n_new
    rowid = lax.broadcasted_iota(jnp.int32, (n_keys, LANES), 0)
    seg = seg_ref[...]

    def per_seq(nb, carry):
        r0 = pl.multiple_of(nb * n_new, n_new)
        rows = pl.ds(r0, n_new)
        for lc in range(wk // LANES):
            lanes = slice(lc * LANES, (lc + 1) * LANES)
            kn = kn_ref[rows, lanes]
            vn = vn_ref[rows, lanes]
            for c, new in ((0, kn), (1, vn)):
                ks_ref[c, lc, 0:win, :] = cache_ref[nb, c, :, lanes]
                ks_ref[c, lc, win:win + n_new, :] = new
                co_ref[nb, c, 0:win - n_new, lanes] = cache_ref[nb, c, n_new:win, lanes]
                co_ref[nb, c, win - n_new:win, lanes] = new
            for r in range(rep):
                cols = slice(r * wk + lc * LANES, r * wk + (lc + 1) * LANES)
                o_l, m_l, l_l = [], [], []
                q8 = q_ref[rows, cols]
                for s in range(n_new):
                    kidx = pl.ds(s, BLK, stride=dil) if dil > 1 else pl.ds(s, BLK)
                    k_all = jnp.concatenate([ks_ref[0, lc, kidx, :], kn], axis=0)
                    v_all = jnp.concatenate([ks_ref[1, lc, kidx, :], vn], axis=0)
                    qs = q8[s:s + 1]
                    prod = (k_all * qs).astype(BF16)
                    sc = jnp.dot(prod, seg, preferred_element_type=F32)
                    sc = jnp.where((rowid < BLK) | (rowid == BLK + s), sc, NEG_INF)
                    m = jnp.max(sc, axis=0, keepdims=True)
                    p = jnp.exp(sc - m)
                    l_l.append(jnp.sum(p, axis=0, keepdims=True))
                    o_l.append(jnp.sum(p * v_all, axis=0, keepdims=True))
                    m_l.append(m)
                o = jnp.concatenate(o_l, axis=0)
                m = jnp.concatenate(m_l, axis=0)
                l = jnp.concatenate(l_l, axis=0)
                if mode == "sink":
                    out_ref[rows, cols] = _with_sink(o, m, l, sink_ref[:, cols])
                elif mode == "merge":
                    out_ref[rows, cols] = _merge3(o, m, l, on1_ref[rows, cols], ls1_ref[rows, cols],
                                                  on2_ref[rows, cols], ls2_ref[rows, cols])
                else:
                    on_ref[rows, cols] = o / l
                    ls_ref[rows, cols] = m + jnp.log(l)
        return carry

    lax.fori_loop(0, nb_seq, per_seq, 0)


def sample_attn(z, cache, layer, *, row0, n_new, dil, qoff, koff, voff, hkv, rep, mode, seg,
                nb_seq, prev_cache=None, extras=(), sink=None, merged_into=()):
    m_rows, nz = z.shape
    n_layers, ns, _, win, wk = cache.shape
    wq = rep * wk
    assert win == dil * BLK and ns % nb_seq == 0 and row0 % (nb_seq * n_new) == 0
    tb = nb_seq * n_new
    rb0 = row0 // tb
    cq, ck, cv = qoff // wq, koff // wk, voff // wk

    in_specs = [
        pl.BlockSpec((tb, wq), lambda i: (rb0 + i, cq)),
        pl.BlockSpec((tb, wk), lambda i: (rb0 + i, ck)),
        pl.BlockSpec((tb, wk), lambda i: (rb0 + i, cv)),
        pl.BlockSpec((None, nb_seq, 2, win, wk), lambda i: (layer, i, 0, 0, 0)),
        _const_spec(seg.shape),
    ]
    args = [z, z, z, cache, seg]
    if mode == "sink":
        in_specs.append(_const_spec(sink.shape))
        args.append(sink)
    elif mode == "merge":
        for e in extras:
            in_specs.append(pl.BlockSpec((tb, wq), lambda i: (rb0 + i, 0)))
            args.append(e)
    aliases = {}
    alias_args = ([prev_cache] if prev_cache is not None else []) + list(merged_into)
    first_out = 0 if prev_cache is not None else 1
    for a_i, a in enumerate(alias_args):
        in_specs.append(pl.BlockSpec(memory_space=pl.ANY))
        aliases[len(args)] = first_out + a_i
        args.append(a)

    tok_block = pl.BlockSpec((tb, wq), lambda i: (rb0 + i, 0))
    tok_shape = jax.ShapeDtypeStruct((m_rows, wq), F32)
    n_tok_out = 2 if mode == "lse" else 1
    assert len(merged_into) == n_tok_out
    out_specs = [pl.BlockSpec((None, nb_seq, 2, win, wk), lambda i: (layer, i, 0, 0, 0))] + [tok_block] * n_tok_out
    out_shape = [jax.ShapeDtypeStruct(cache.shape, F32)] + [tok_shape] * n_tok_out

    res = pl.pallas_call(
        functools.partial(_sample_attn_kernel, nb_seq=nb_seq, win=win, dil=dil, hkv=hkv, rep=rep, mode=mode,
                          n_alias=len(alias_args), n_new=n_new),
        grid=(ns // nb_seq,),
        in_specs=in_specs, out_specs=out_specs, out_shape=out_shape,
        scratch_shapes=[pltpu.VMEM((2, wk // LANES, win + n_new, LANES), F32)],
        input_output_aliases=aliases,
        compiler_params=pltpu.CompilerParams(dimension_semantics=("parallel",), vmem_limit_bytes=VMEM_LIMIT),
        name=f"sample_attn_d{dil}_{mode}",
    )(*args)
    return res[0], tuple(res[1:])


def _s5_kernel(*refs, tm, n_cplx, per_block):
    (u_ref, b0_ref, b1_ref, c0_ref, c1_ref, are_ref, aim_ref, d_ref, wglu_ref, bglu_ref) = refs[:10]
    if per_block:
        h0_ref, ob_ref, st_ref, hs_ref = refs[10:]
    else:
        ob_ref, st_ref, hs_ref, carry_ref = refs[10:]

        @pl.when(pl.program_id(1) == 0)
        def _():
            carry_ref[...] = jnp.zeros_like(carry_ref)

    half = n_cplx // 2
    u = u_ref[...]
    ub = u.astype(BF16)
    dh = u.shape[1] // 2
    hs_ref[:, 0:2 * half] = jnp.dot(ub[:, :dh], b0_ref[...], preferred_element_type=F32)
    hs_ref[:, 2 * half:4 * half] = jnp.dot(ub[:, dh:], b1_ref[...], preferred_element_type=F32)

    n_blk = tm // SUBLANES
    cols_per_loop = 2
    for c0 in range(0, n_cplx // LANES, cols_per_loop):
        consts, lanes = [], []
        for c in range(c0, c0 + cols_per_loop):
            sl = slice(c * LANES, (c + 1) * LANES)
            ar, ai = are_ref[:, sl], aim_ref[:, sl]
            consts.append([(ar[k * 8:(k + 1) * 8], ai[k * 8:(k + 1) * 8]) for k in range(4)])
            base = (c * LANES // half) * 2 * half + (c * LANES) % half
            lanes.append((slice(base, base + LANES), slice(base + half, base + half + LANES)))

        def scan8(k, ci, cr, cim, consts=consts, lanes=lanes):
            rows = pl.ds(pl.multiple_of(k * SUBLANES, SUBLANES), SUBLANES)
            lr, li = lanes[ci]
            xr, xi = hs_ref[rows, lr], hs_ref[rows, li]
            for step, sh in enumerate((1, 2, 4)):
                ar, ai = consts[ci][step]
                sr, si = pltpu.roll(xr, sh, 0), pltpu.roll(xi, sh, 0)
                xr, xi = xr + ar * sr - ai * si, xi + ar * si + ai * sr
            pr, pi = consts[ci][3]
            xr, xi = xr + pr * cr - pi * cim, xi + pr * cim + pi * cr
            hs_ref[rows, lr] = xr
            hs_ref[rows, li] = xi
            return xr[SUBLANES - 1:SUBLANES], xi[SUBLANES - 1:SUBLANES]

        def blk(k, carry):
            return tuple(scan8(k, ci, *carry[ci]) for ci in range(cols_per_loop))

        def blk_own_state(kk, carry, lanes=lanes):
            grp = pl.ds(pl.multiple_of(kk * SUBLANES, SUBLANES), SUBLANES)
            for ci in range(cols_per_loop):
                h0r, h0i = h0_ref[grp, lanes[ci][0]], h0_ref[grp, lanes[ci][1]]
                last = [scan8(kk * SUBLANES + jj, ci, h0r[jj:jj + 1], h0i[jj:jj + 1]) for jj in range(SUBLANES)]
                st_ref[grp, lanes[ci][0]] = jnp.concatenate([t[0] for t in last], axis=0)
                st_ref[grp, lanes[ci][1]] = jnp.concatenate([t[1] for t in last], axis=0)
            return carry

        if per_block:
            lax.fori_loop(0, n_blk // SUBLANES, blk_own_state, 0)
        else:
            init = tuple((carry_ref[:, lanes[ci][0]], carry_ref[:, lanes[ci][1]]) for ci in range(cols_per_loop))
            fin = lax.fori_loop(0, n_blk, blk, init, unroll=2)
            for ci in range(cols_per_loop):
                carry_ref[:, lanes[ci][0]] = fin[ci][0]
                carry_ref[:, lanes[ci][1]] = fin[ci][1]

    if not per_block:
        st_ref[...] = carry_ref[...]

    y_lo = jnp.dot(hs_ref[:, 0:2 * half].astype(BF16), c0_ref[...], preferred_element_type=F32)
    y_hi = jnp.dot(hs_ref[:, 2 * half:4 * half].astype(BF16), c1_ref[...], preferred_element_type=F32)
    y = jnp.concatenate([y_lo, y_hi], axis=1) + d_ref[...] * u
    zb = jax.nn.gelu(y)
    gate = jnp.dot(zb.astype(BF16), wglu_ref[...], preferred_element_type=F32) + bglu_ref[...]
    ob_ref[...] = zb * jax.nn.sigmoid(gate)


def s5_layer(z, sp, *, uoff, row0, n_seq, seq, h0=None, merged_into=None):
    m_rows, _ = z.shape
    db = sp["d"].shape[1]
    n_cplx = sp["a_re"].shape[1]
    per_block = h0 is not None
    cu = uoff // db
    consts = [sp["b0"], sp["b1"], sp["c0"], sp["c1"], sp["a_re"], sp["a_im"], sp["d"], sp["w_glu"], sp["b_glu"]]
    if per_block:
        assert seq == SUBLANES
        rows = n_seq * seq
        tm = _pick(rows, (256, 128, 64))
        rb0 = row0 // tm
        grid = (rows // tm,)
        in_specs = [pl.BlockSpec((tm, db), lambda i: (rb0 + i, cu))] + [_const_spec(c.shape) for c in consts]
        in_specs.append(pl.BlockSpec((tm // SUBLANES, 2 * n_cplx), lambda i: (i, 0)))
        args = [z] + consts + [h0]
        out_specs = [pl.BlockSpec((tm, db), lambda i: (rb0 + i, 0)),
                     pl.BlockSpec((tm // SUBLANES, 2 * n_cplx), lambda i: (i, 0))]
        st_shape = jax.ShapeDtypeStruct((n_seq, 2 * n_cplx), F32)
        scratch = [pltpu.VMEM((tm, 2 * n_cplx), F32)]
        sem = ("arbitrary",)
    else:
        tm = _pick(seq, (256, 128))
        nt = seq // tm
        rb0 = row0 // tm
        grid = (n_seq, nt)
        in_specs = [pl.BlockSpec((tm, db), lambda n, t: (rb0 + n * nt + t, cu))] + [_const_spec(c.shape) for c in consts]
        args = [z] + consts
        out_specs = [pl.BlockSpec((tm, db), lambda n, t: (rb0 + n * nt + t, 0)),
                     pl.BlockSpec((None, 1, 2 * n_cplx), lambda n, t: (n, 0, 0))]
        st_shape = jax.ShapeDtypeStruct((n_seq, 1, 2 * n_cplx), F32)
        scratch = [pltpu.VMEM((tm, 2 * n_cplx), F32), pltpu.VMEM((1, 2 * n_cplx), F32)]
        sem = ("arbitrary", "arbitrary")
    aliases = {}
    if merged_into is not None:
        in_specs.append(pl.BlockSpec(memory_space=pl.ANY))
        aliases[len(args)] = 0
        args.append(merged_into)
    ob, st = pl.pallas_call(
        functools.partial(_s5_kernel_entry, tm=tm, n_cplx=n_cplx, per_block=per_block,
                          n_alias=len(aliases)),
        grid=grid, in_specs=in_specs, out_specs=out_specs,
        out_shape=[jax.ShapeDtypeStruct((m_rows, db), F32), st_shape],
        scratch_shapes=scratch, input_output_aliases=aliases,
        compiler_params=pltpu.CompilerParams(dimension_semantics=sem, vmem_limit_bytes=VMEM_LIMIT),
        name="s5_sample" if per_block else "s5_prompt",
    )(*args)
    return ob, st.reshape(n_seq, 2 * n_cplx)


def _s5_kernel_entry(*refs, tm, n_cplx, per_block, n_alias):
    n_in = 11 if per_block else 10
    refs = refs[:n_in] + refs[n_in + n_alias:]
    _s5_kernel(*refs, tm=tm, n_cplx=n_cplx, per_block=per_block)


def _cmul(a, b):
    return a[0] * b[0] - a[1] * b[1], a[0] * b[1] + a[1] * b[0]


def s5_params(a_re, a_im, log_dt, b_re, b_im, c_re, c_im, d_skip, w_glu, b_glu):
    g_b, n_s = a_re.shape
    gh = g_b // 2
    hp = lax.Precision.HIGHEST
    lam = lax.complex(a_re.astype(F32), a_im.astype(F32))
    dt = jnp.exp(log_dt.astype(F32))[:, None]
    abar = jnp.exp(lam * dt)
    bbar = ((abar - 1.0) / lam)[..., None] * lax.complex(b_re.astype(F32), b_im.astype(F32))
    eye = jnp.eye(gh, dtype=F32)

    def in_map(half):
        bb = bbar[half * gh:(half + 1) * gh]
        blocks = [jnp.einsum("gsc,gh->gchs", part, eye, precision=hp).reshape(gh * GROUP_CH, gh * n_s)
                  for part in (bb.real, bb.imag)]
        return jnp.concatenate(blocks, axis=1).astype(BF16)

    def out_map(half):
        blocks = [jnp.einsum("gcs,gh->gshc", part[half * gh:(half + 1) * gh], eye,
                             precision=hp).reshape(gh * n_s, gh * GROUP_CH)
                  for part in (c_re.astype(F32), -c_im.astype(F32))]
        return jnp.concatenate(blocks, axis=0).astype(BF16)

    a1 = (abar.real.reshape(1, g_b * n_s), abar.imag.reshape(1, g_b * n_s))
    pw = [a1]
    for _ in range(SUBLANES - 1):
        pw.append(_cmul(pw[-1], a1))
    row = jnp.arange(SUBLANES)[:, None]
    tabs_re, tabs_im = [], []
    for sh in (1, 2, 4):
        tabs_re.append(jnp.where(row >= sh, pw[sh - 1][0], 0.0))
        tabs_im.append(jnp.where(row >= sh, pw[sh - 1][1], 0.0))
    tabs_re.append(jnp.concatenate([p[0] for p in pw], axis=0))
    tabs_im.append(jnp.concatenate([p[1] for p in pw], axis=0))
    return dict(b0=in_map(0), b1=in_map(1), c0=out_map(0), c1=out_map(1),
                a_re=jnp.concatenate(tabs_re, axis=0), a_im=jnp.concatenate(tabs_im, axis=0),
                d=d_skip.astype(F32).reshape(1, -1), w_glu=w_glu.astype(BF16),
                b_glu=b_glu.astype(F32).reshape(1, -1))


def _state_to_lanes(re, im):
    n, g_b, n_s = re.shape
    gh = g_b // 2
    parts = []
    for half in range(2):
        for x in (re, im):
            parts.append(x[:, half * gh:(half + 1) * gh].reshape(n, gh * n_s))
    return jnp.concatenate(parts, axis=1).astype(F32)


def _lanes_to_state(st, g_b, n_s):
    n = st.shape[0]
    gh = g_b // 2
    st = st.reshape(n, 2, 2, gh, n_s)
    re = st[:, :, 0].reshape(n, g_b, n_s)
    im = st[:, :, 1].reshape(n, g_b, n_s)
    return re, im


def _post_kernel(*refs, n_a, ff_chunk):
    x_ref = refs[0]
    a_refs = refs[1:1 + n_a]
    p_ref = refs[1 + n_a]
    wo_refs = refs[2 + n_a:2 + 2 * n_a]
    (g_pm, g_pf, w1_ref, w2_ref, g_qf, g_ple, wg_ref, wp_ref, o_ref) = refs[2 + 2 * n_a:]

    x = x_ref[...]
    mix = None
    for a_ref, w_ref in zip(a_refs, wo_refs):
        t = jnp.dot(a_ref[...].astype(BF16), w_ref[...], preferred_element_type=F32)
        mix = t if mix is None else mix + t
    x = x + _rms(mix, g_pm[...])
    hf = _rms(x, g_pf[...]).astype(BF16)
    d_ff = w1_ref.shape[1]
    f = None
    for c in range(0, d_ff, ff_chunk):
        t = jnp.dot(hf, w1_ref[:, c:c + ff_chunk], preferred_element_type=F32)
        t = jnp.square(jnp.maximum(t, 0.0)).astype(BF16)
        t = jnp.dot(t, w2_ref[c:c + ff_chunk, :], preferred_element_type=F32)
        f = t if f is None else f + t
    x = x + _rms(f, g_qf[...])
    gate = jax.nn.sigmoid(jnp.dot(_rms(x, g_ple[...]).astype(BF16), wg_ref[...], preferred_element_type=F32))
    pe = jnp.dot(p_ref[...].astype(BF16), wp_ref[...], preferred_element_type=F32)
    o_ref[...] = x + pe * gate


def post_block(x, a_list, wo_list, p_all, layer, g_pm, g_pf, w1, w2, g_qf, g_ple, wg, wp):
    m, d = x.shape
    tm = _pick(m, (512, 256, 128))
    n_a = len(a_list)
    consts = list(wo_list) + [g_pm, g_pf, w1, w2, g_qf, g_ple, wg, wp]
    in_specs = [pl.BlockSpec((tm, d), lambda i: (i, 0))]
    in_specs += [pl.BlockSpec((tm, a.shape[1]), lambda i: (i, 0)) for a in a_list]
    in_specs += [pl.BlockSpec((None, tm, p_all.shape[2]), lambda i: (layer, i, 0))]
    in_specs += [pl.BlockSpec(c.shape, lambda i, nd=c.ndim: (0,) * nd, pipeline_mode=pl.Buffered(1)) for c in consts]
    return pl.pallas_call(
        functools.partial(_post_kernel, n_a=n_a, ff_chunk=min(1024, w1.shape[1])),
        grid=(m // tm,),
        in_specs=in_specs,
        out_specs=pl.BlockSpec((tm, d), lambda i: (i, 0)),
        out_shape=jax.ShapeDtypeStruct((m, d), F32),
        compiler_params=pltpu.CompilerParams(dimension_semantics=("parallel",), vmem_limit_bytes=VMEM_LIMIT),
        name="post_block",
    )(x, *a_list, p_all, *consts)


def _seg_matrix(width):
    h = jnp.arange(width) // HEAD_DIM
    return jnp.where(h[:, None] == h[None, :], HEAD_DIM ** -0.5, 0.0).astype(BF16)


def kernel(x_prompt, x_sample, cache_a1_kv, cache_a2_kv, cache_a3_kv, state_b_re, state_b_im, cache_c_kv, p_prompt, p_sample, w_in_ab, w_out_ab, ssm_a_re, ssm_a_im, ssm_log_dt, ssm_b_re, ssm_b_im, ssm_c_re, ssm_c_im, ssm_d, w_glu, b_glu, w_in_c, sinks_c, w_out_c, g_pre_mix, g_post_mix, g_pre_ffn, g_post_ffn, g_ple, w_ff1, w_ff2, w_ple, w_ple_gate):
    batch, seq, d_model = x_prompt.shape
    ns, n_new, _ = x_sample.shape
    depth = g_pre_mix.shape[0]
    n_pt = batch * seq
    n_st = ns * n_new
    h_a = d_model // (2 * HEAD_DIM)
    d_a = h_a * HEAD_DIM
    n_dil = len(DILATED)
    qkv_a = n_dil * 3 * d_a
    g_b, n_s = ssm_a_re.shape[1:]
    h_c = d_model // HEAD_DIM
    kv_c = cache_c_kv.shape[4]
    rep_c = h_c // kv_c
    wk_c = kv_c * HEAD_DIM

    x = jnp.concatenate([x_prompt.reshape(n_pt, d_model), x_sample.reshape(n_st, d_model)], axis=0)
    p_all = jnp.concatenate([p_prompt.reshape(depth, n_pt, -1), p_sample.reshape(depth, n_st, -1)], axis=1)

    caches_a = [c.reshape(c.shape[:4] + (d_a,)) for c in (cache_a1_kv, cache_a2_kv, cache_a3_kv)]
    cache_c = cache_c_kv.reshape(cache_c_kv.shape[:4] + (wk_c,))
    seg_a = seg_c = _seg_matrix(LANES)
    nb_seq_a = (_pick(ns, (8, 4, 2, 1)), _pick(ns, (4, 2, 1)), 1)
    nb_seq_c = _pick(ns, (8, 4, 2, 1))

    perm = jnp.arange(h_c * HEAD_DIM).reshape(kv_c, rep_c, HEAD_DIM).transpose(1, 0, 2).reshape(-1)

    def row_g(v):
        return v.astype(F32).reshape(1, -1)

    new_a = [None] * n_dil
    new_c = None
    a_prompt = [[] for _ in range(n_dil)]
    c_prompt = []
    st_prompt, st_sample = [], []

    for i in range(depth):
        j = i // 2
        if i % 2 == 0:
            z = norm_matmul(x, row_g(g_pre_mix[i]), w_in_ab[j].astype(BF16))
            zp = z[:n_pt].reshape(batch, seq, -1)
            for g, (win, dil) in enumerate(DILATED):
                keep = min(win, seq)
                k = zp[:, seq - keep:, g * 3 * d_a + d_a:g * 3 * d_a + 2 * d_a]
                v = zp[:, seq - keep:, g * 3 * d_a + 2 * d_a:g * 3 * d_a + 3 * d_a]
                a_prompt[g].append(jnp.stack([k, v], axis=1).reshape(batch, 2, keep, h_a, HEAD_DIM))
            pairs = []
            for g in (2, 1, 0):
                win, dil = DILATED[g]
                off = g * 3 * d_a
                common = dict(dil=dil, qoff=off, koff=off + d_a, voff=off + 2 * d_a)
                if g > 0:
                    on, ls = prompt_attn(z, batch=batch, seq=seq, hq=h_a, hkv=h_a, mode="lse", **common)
                    new_a[g], (on, ls) = sample_attn(
                        z, caches_a[g], j, row0=n_pt, n_new=n_new, hkv=h_a, rep=1, mode="lse", seg=seg_a,
                        nb_seq=nb_seq_a[g], prev_cache=new_a[g], merged_into=(on, ls), **common)
                    pairs += [on, ls]
                else:
                    out_a = prompt_attn(z, batch=batch, seq=seq, hq=h_a, hkv=h_a, mode="merge", extras=pairs,
                                        **common)
                    new_a[g], (out_a,) = sample_attn(
                        z, caches_a[g], j, row0=n_pt, n_new=n_new, hkv=h_a, rep=1, mode="merge", seg=seg_a,
                        nb_seq=nb_seq_a[g], prev_cache=new_a[g], extras=pairs, merged_into=(out_a,), **common)
            sp = s5_params(ssm_a_re[j], ssm_a_im[j], ssm_log_dt[j], ssm_b_re[j], ssm_b_im[j], ssm_c_re[j],
                           ssm_c_im[j], ssm_d[j], w_glu[j], b_glu[j])
            out_b, st_p = s5_layer(z, sp, uoff=qkv_a, row0=0, n_seq=batch, seq=seq)
            out_b, st_s = s5_layer(z, sp, uoff=qkv_a, row0=n_pt, n_seq=ns, seq=n_new,
                                   h0=_state_to_lanes(state_b_re[j], state_b_im[j]), merged_into=out_b)
            st_prompt.append(st_p)
            st_sample.append(st_s)
            wo = w_out_ab[j].astype(BF16)
            a_list, wo_list = [out_a, out_b], [wo[:d_a], wo[d_a:]]
        else:
            w_in = w_in_c[j]
            w_in = jnp.concatenate([w_in[:, :h_c * HEAD_DIM][:, perm], w_in[:, h_c * HEAD_DIM:]], axis=1)
            z = norm_matmul(x, row_g(g_pre_mix[i]), w_in.astype(BF16))
            qw = h_c * HEAD_DIM
            zp = z[:n_pt].reshape(batch, seq, -1)
            keep = min(WIN_C, seq)
            c_prompt.append(jnp.stack([zp[:, seq - keep:, qw:qw + wk_c], zp[:, seq - keep:, qw + wk_c:]],
                                      axis=1).reshape(batch, 2, keep, kv_c, HEAD_DIM))
            sink = jnp.broadcast_to(sinks_c[j].astype(F32).reshape(kv_c, rep_c, 1).transpose(1, 0, 2),
                                    (rep_c, kv_c, HEAD_DIM)).reshape(1, qw)
            common = dict(dil=1, qoff=0, koff=qw, voff=qw + wk_c, mode="sink", sink=sink)
            att = prompt_attn(z, batch=batch, seq=seq, hq=h_c, hkv=kv_c, **common)
            new_c, (att,) = sample_attn(z, cache_c, j, row0=n_pt, n_new=n_new, hkv=kv_c, rep=rep_c, seg=seg_c,
                                        nb_seq=nb_seq_c, prev_cache=new_c, merged_into=(att,), **common)
            a_list, wo_list = [att], [w_out_c[j][perm].astype(BF16)]
        x = post_block(x, a_list, wo_list, p_all, i, row_g(g_post_mix[i]), row_g(g_pre_ffn[i]),
                       w_ff1[i].astype(BF16), w_ff2[i].astype(BF16), row_g(g_post_ffn[i]), row_g(g_ple[i]),
                       w_ple_gate[i].astype(BF16), w_ple[i].astype(BF16))

    y_prompt = x[:n_pt].reshape(batch, seq, d_model)
    y_sample = x[n_pt:].reshape(ns, n_new, d_model)
    outs_a = []
    for g in range(n_dil):
        outs_a.append(jnp.stack(a_prompt[g]))
        outs_a.append(new_a[g].reshape(new_a[g].shape[:4] + (h_a, HEAD_DIM)))
    sp_re, sp_im = zip(*[_lanes_to_state(s, g_b, n_s) for s in st_prompt])
    ss_re, ss_im = zip(*[_lanes_to_state(s, g_b, n_s) for s in st_sample])
    return (y_prompt, y_sample, *outs_a,
            jnp.stack(sp_re), jnp.stack(ss_re), jnp.stack(sp_im), jnp.stack(ss_im),
            jnp.stack(c_prompt), new_c.reshape(new_c.shape[:4] + (kv_c, HEAD_DIM)))
```

```python
import functools

import jax
import jax.numpy as jnp
from jax import lax
from jax.experimental import pallas as pl
from jax.experimental.pallas import tpu as pltpu

F32 = jnp.float32
BF16 = jnp.bfloat16

HEAD_DIM = 64
BLK = 128
DILATED = ((128, 1), (512, 4), (2048, 16))
GROUP_CH = 16
N_STATE = 64
WIN_C = 128
RMS_EPS = 1e-6
NEG_INF = -1e30
SUBLANES = 8
LANES = 128
VMEM_LIMIT = 48 * 1024 * 1024


def _pick(n, prefs):
    for p in prefs:
        if n % p == 0:
            return p
    raise ValueError(f"no tile in {prefs} divides {n}")


def _rms(x, g):
    ms = jnp.mean(x * x, axis=-1, keepdims=True)
    return x * lax.rsqrt(ms + RMS_EPS) * g


def _const_spec(shape):
    nd = len(shape)
    return pl.BlockSpec(shape, lambda *_: (0,) * nd)


def _norm_matmul_kernel(x_ref, g_ref, w_ref, o_ref, h_ref):
    @pl.when(pl.program_id(1) == 0)
    def _():
        h_ref[...] = _rms(x_ref[...], g_ref[...]).astype(BF16)

    o_ref[...] = jnp.dot(h_ref[...], w_ref[...], preferred_element_type=F32)


def norm_matmul(x, g, w):
    m, d = x.shape
    n = w.shape[1]
    tm = _pick(m, (1024, 512, 256, 128))
    tn = _pick(n, (512, 256, 128))
    return pl.pallas_call(
        _norm_matmul_kernel,
        grid=(m // tm, n // tn),
        in_specs=[pl.BlockSpec((tm, d), lambda i, j: (i, 0)),
                  pl.BlockSpec((1, d), lambda i, j: (0, 0)),
                  pl.BlockSpec((d, tn), lambda i, j: (0, j))],
        out_specs=pl.BlockSpec((tm, tn), lambda i, j: (i, j)),
        out_shape=jax.ShapeDtypeStruct((m, n), F32),
        scratch_shapes=[pltpu.VMEM((tm, d), BF16)],
        compiler_params=pltpu.CompilerParams(dimension_semantics=("parallel", "arbitrary"),
                                             vmem_limit_bytes=VMEM_LIMIT),
        name="norm_matmul",
    )(x, g, w)


def _merge3(o, m, l, on1, ls1, on2, ls2):
    ls0 = m + jnp.log(l)
    mx = jnp.maximum(jnp.maximum(ls0, ls1), ls2)
    w0 = jnp.exp(ls0 - mx)
    w1 = jnp.exp(ls1 - mx)
    w2 = jnp.exp(ls2 - mx)
    return (w0 * (o / l) + w1 * on1 + w2 * on2) / (w0 + w1 + w2)


def _with_sink(o, m, l, sk):
    mx = jnp.maximum(m, sk)
    sc = jnp.exp(m - mx)
    return o * sc / (l * sc + jnp.exp(sk - mx))


def _prompt_attn_kernel(*refs, hq, hkv, mode):
    q_ref, kp_ref, kc_ref, vp_ref, vc_ref = refs[:5]
    rest = refs[5:]
    if mode == "sink":
        sink_ref, out_ref = rest
    elif mode == "merge":
        on1_ref, ls1_ref, on2_ref, ls2_ref, out_ref = rest
    else:
        on_ref, ls_ref = rest

    b = pl.program_id(2)
    qi = lax.broadcasted_iota(jnp.int32, (BLK, 2 * BLK), 0)
    ki = lax.broadcasted_iota(jnp.int32, (BLK, 2 * BLK), 1)
    valid = (ki >= qi) & (ki <= qi + BLK) & ((ki >= BLK) | (b > 0))

    kv = {}
    for h in range(hq):
        g = h % hkv
        if g not in kv:
            sl = slice(g * HEAD_DIM, (g + 1) * HEAD_DIM)
            k2 = jnp.concatenate([kp_ref[:, sl], kc_ref[:, sl]], axis=0).astype(BF16)
            v2 = jnp.concatenate([vp_ref[:, sl], vc_ref[:, sl]], axis=0).astype(BF16)
            kv[g] = (k2, v2)
        k2, v2 = kv[g]
        hs = slice(h * HEAD_DIM, (h + 1) * HEAD_DIM)
        qh = q_ref[:, hs].astype(BF16)
        s = lax.dot_general(qh, k2, (((1,), (1,)), ((), ())), preferred_element_type=F32)
        s = jnp.where(valid, s * (HEAD_DIM ** -0.5), NEG_INF)
        m = jnp.max(s, axis=-1, keepdims=True)
        p = jnp.exp(s - m)
        l = jnp.sum(p, axis=-1, keepdims=True)
        o = jnp.dot(p.astype(BF16), v2, preferred_element_type=F32)
        if mode == "sink":
            out_ref[:, hs] = _with_sink(o, m, l, sink_ref[:, hs])
        elif mode == "merge":
            out_ref[:, hs] = _merge3(o, m, l, on1_ref[:, hs], ls1_ref[:, hs], on2_ref[:, hs], ls2_ref[:, hs])
        else:
            on_ref[:, hs] = o / l
            ls_ref[:, hs] = jnp.broadcast_to(m + jnp.log(l), (BLK, HEAD_DIM))


def prompt_attn(z, *, batch, seq, dil, qoff, koff, voff, hq, hkv, mode, extras=(), sink=None):
    m_rows, nz = z.shape
    wq, wk = hq * HEAD_DIM, hkv * HEAD_DIM
    nbk = seq // dil // BLK
    zv = z.reshape(m_rows // dil, dil * nz)
    cq, ck, cv = qoff // wq, koff // wk, voff // wk

    def row(n, b):
        return n * nbk + b

    in_specs = [
        pl.BlockSpec((BLK, wq), lambda n, r, b: (row(n, b), r * (nz // wq) + cq)),
        pl.BlockSpec((BLK, wk), lambda n, r, b: (row(n, jnp.maximum(b - 1, 0)), r * (nz // wk) + ck)),
        pl.BlockSpec((BLK, wk), lambda n, r, b: (row(n, b), r * (nz // wk) + ck)),
        pl.BlockSpec((BLK, wk), lambda n, r, b: (row(n, jnp.maximum(b - 1, 0)), r * (nz // wk) + cv)),
        pl.BlockSpec((BLK, wk), lambda n, r, b: (row(n, b), r * (nz // wk) + cv)),
    ]
    args = [zv, zv, zv, zv, zv]
    out_block = pl.BlockSpec((BLK, wq), lambda n, r, b: (row(n, b), r))
    out_one = jax.ShapeDtypeStruct((m_rows // dil, dil * wq), F32)
    if mode == "sink":
        in_specs.append(pl.BlockSpec((1, wq), lambda n, r, b: (0, 0)))
        args.append(sink)
        out_specs, out_shape = out_block, out_one
    elif mode == "merge":
        assert dil == 1
        for e in extras:
            in_specs.append(pl.BlockSpec((BLK, wq), lambda n, r, b: (row(n, b), 0)))
            args.append(e)
        out_specs, out_shape = out_block, out_one
    else:
        out_specs, out_shape = [out_block, out_block], [out_one, out_one]

    res = pl.pallas_call(
        functools.partial(_prompt_attn_kernel, hq=hq, hkv=hkv, mode=mode),
        grid=(batch, dil, nbk),
        in_specs=in_specs, out_specs=out_specs, out_shape=out_shape,
        compiler_params=pltpu.CompilerParams(dimension_semantics=("parallel", "parallel", "arbitrary"),
                                             vmem_limit_bytes=VMEM_LIMIT),
        name=f"prompt_attn_d{dil}_{mode}",
    )(*args)
    if mode in ("sink", "merge"):
        return res.reshape(m_rows, wq)
    return res[0].reshape(m_rows, wq), res[1].reshape(m_rows, wq)


def _sample_attn_kernel(*refs, nb_seq, win, dil, hkv, rep, mode, n_alias, n_new):
    wk = hkv * HEAD_DIM
    q_ref, kn_ref, vn_ref, cache_ref, seg_ref = refs[:5]
    pos = 5
    if mode == "sink":
        sink_ref = refs[pos]
        pos += 1
    elif mode == "merge":
        on1_ref, ls1_ref, on2_ref, ls2_ref = refs[pos:pos + 4]
        pos += 4
    pos += n_alias
    co_ref = refs[pos]
    pos += 1
    if mode == "lse":
        on_ref, ls_ref = refs[pos:pos + 2]
        pos += 2
    else:
        out_ref = refs[pos]
        pos += 1
    ks_ref = refs[pos]

    n_keys = BLK + n_new
    rowid = lax.broadcasted_iota(jnp.int32, (n_keys, LANES), 0)
    seg = seg_ref[...]

    def per_seq(nb, carry):
        r0 = pl.multiple_of(nb * n_new, n_new)
        rows = pl.ds(r0, n_new)
        for lc in range(wk // LANES):
            lanes = slice(lc * LANES, (lc + 1) * LANES)
            kn = kn_ref[rows, lanes]
            vn = vn_ref[rows, lanes]
            for c, new in ((0, kn), (1, vn)):
                ks_ref[c, lc, 0:win, :] = cache_ref[nb, c, :, lanes]
                ks_ref[c, lc, win:win + n_new, :] = new
                co_ref[nb, c, 0:win - n_new, lanes] = cache_ref[nb, c, n_new:win, lanes]
                co_ref[nb, c, win - n_new:win, lanes] = new
            for r in range(rep):
                cols = slice(r * wk + lc * LANES, r * wk + (lc + 1) * LANES)
                o_l, m_l, l_l = [], [], []
                q8 = q_ref[rows, cols]
                for s in range(n_new):
                    kidx = pl.ds(s, BLK, stride=dil) if dil > 1 else pl.ds(s, BLK)
                    k_all = jnp.concatenate([ks_ref[0, lc, kidx, :], kn], axis=0)
                    v_all = jnp.concatenate([ks_ref[1, lc, kidx, :], vn], axis=0)
                    qs = q8[s:s + 1]
                    prod = (k_all * qs).astype(BF16)
                    sc = jnp.dot(prod, seg, preferred_element_type=F32)
                    sc = jnp.where((rowid < BLK) | (rowid == BLK + s), sc, NEG_INF)
                    m = jnp.max(sc, axis=0, keepdims=True)
                    p = jnp.exp(sc - m)
                    l_l.append(jnp.sum(p, axis=0, keepdims=True))
                    o_l.append(jnp.sum(p * v_all, axis=0, keepdims=True))
                    m_l.append(m)
                o = jnp.concatenate(o_l, axis=0)
                m = jnp.concatenate(m_l, axis=0)
                l = jnp.concatenate(l_l, axis=0)
                if mode == "sink":
                    out_ref[rows, cols] = _with_sink(o, m, l, sink_ref[:, cols])
                elif mode == "merge":
                    out_ref[rows, cols] = _merge3(o, m, l, on1_ref[rows, cols], ls1_ref[rows, cols],
                                                  on2_ref[rows, cols], ls2_ref[rows, cols])
                else:
                    on_ref[rows, cols] = o / l
                    ls_ref[rows, cols] = m + jnp.log(l)
        return carry

    lax.fori_loop(0, nb_seq, per_seq, 0)


def sample_attn(z, cache, layer, *, row0, n_new, dil, qoff, koff, voff, hkv, rep, mode, seg,
                nb_seq, prev_cache=None, extras=(), sink=None, merged_into=()):
    m_rows, nz = z.shape
    n_layers, ns, _, win, wk = cache.shape
    wq = rep * wk
    assert win == dil * BLK and ns % nb_seq == 0 and row0 % (nb_seq * n_new) == 0
    tb = nb_seq * n_new
    rb0 = row0 // tb
    cq, ck, cv = qoff // wq, koff // wk, voff // wk

    in_specs = [
        pl.BlockSpec((tb, wq), lambda i: (rb0 + i, cq)),
        pl.BlockSpec((tb, wk), lambda i: (rb0 + i, ck)),
        pl.BlockSpec((tb, wk), lambda i: (rb0 + i, cv)),
        pl.BlockSpec((None, nb_seq, 2, win, wk), lambda i: (layer, i, 0, 0, 0)),
        _const_spec(seg.shape),
    ]
    args = [z, z, z, cache, seg]
    if mode == "sink":
        in_specs.append(_const_spec(sink.shape))
        args.append(sink)
    elif mode == "merge":
        for e in extras:
            in_specs.append(pl.BlockSpec((tb, wq), lambda i: (rb0 + i, 0)))
            args.append(e)
    aliases = {}
    alias_args = ([prev_cache] if prev_cache is not None else []) + list(merged_into)
    first_out = 0 if prev_cache is not None else 1
    for a_i, a in enumerate(alias_args):
        in_specs.append(pl.BlockSpec(memory_space=pl.ANY))
        aliases[len(args)] = first_out + a_i
        args.append(a)

    tok_block = pl.BlockSpec((tb, wq), lambda i: (rb0 + i, 0))
    tok_shape = jax.ShapeDtypeStruct((m_rows, wq), F32)
    n_tok_out = 2 if mode == "lse" else 1
    assert len(merged_into) == n_tok_out
    out_specs = [pl.BlockSpec((None, nb_seq, 2, win, wk), lambda i: (layer, i, 0, 0, 0))] + [tok_block] * n_tok_out
    out_shape = [jax.ShapeDtypeStruct(cache.shape, F32)] + [tok_shape] * n_tok_out

    res = pl.pallas_call(
        functools.partial(_sample_attn_kernel, nb_seq=nb_seq, win=win, dil=dil, hkv=hkv, rep=rep, mode=mode,
                          n_alias=len(alias_args), n_new=n_new),
        grid=(ns // nb_seq,),
        in_specs=in_specs, out_specs=out_specs, out_shape=out_shape,
        scratch_shapes=[pltpu.VMEM((2, wk // LANES, win + n_new, LANES), F32)],
        input_output_aliases=aliases,
        compiler_params=pltpu.CompilerParams(dimension_semantics=("parallel",), vmem_limit_bytes=VMEM_LIMIT),
        name=f"sample_attn_d{dil}_{mode}",
    )(*args)
    return res[0], tuple(res[1:])


def _window_attn_kernel(*refs, layer, win, dil, hkv, mode, n_new):
    q_ref, kn_ref, vn_ref, seg_ref = refs[:4]
    pos = 4
    if mode == "merge":
        on1_ref, ls1_ref, on2_ref, ls2_ref = refs[pos:pos + 4]
        pos += 4
    cache_hbm = refs[pos]
    if mode == "lse":
        co_hbm, on_ref, ls_ref, slot, in_sem, out_sem = refs[-6:]
    else:
        co_hbm, out_ref, slot, in_sem, out_sem = refs[-5:]

    wd = win // dil
    n = pl.program_id(0)
    n_steps = pl.num_programs(0)
    cur = n % 2
    nxt = 1 - cur

    def in_copy(seq, s_):
        return pltpu.make_async_copy(cache_hbm.at[layer, seq], slot.at[s_, :, pl.ds(0, wd)], in_sem.at[s_])

    def out_copies(seq, s_):
        if n_new % dil == 0:
            return [pltpu.make_async_copy(slot.at[s_, :, pl.ds(n_new // dil, wd)], co_hbm.at[layer, seq],
                                          out_sem.at[s_])]
        assert dil == 2 * n_new
        lo, hi = pl.ds(0, n_new), pl.ds(n_new, n_new)
        return [pltpu.make_async_copy(slot.at[s_, :, pl.ds(0, wd), hi], co_hbm.at[layer, seq, :, :, lo],
                                      out_sem.at[s_]),
                pltpu.make_async_copy(slot.at[s_, :, pl.ds(1, wd), lo], co_hbm.at[layer, seq, :, :, hi],
                                      out_sem.at[s_])]

    @pl.when(n == 0)
    def _():
        in_copy(0, 0).start()

    @pl.when(n > 0)
    def _():
        for c in out_copies(n - 1, nxt):
            c.wait()

    @pl.when(n + 1 < n_steps)
    def _():
        in_copy(n + 1, nxt).start()

    in_copy(n, cur).wait()
    for s in range(n_new):
        slot[cur, 0, wd + s // dil, s % dil] = kn_ref[s]
        slot[cur, 1, wd + s // dil, s % dil] = vn_ref[s]

    seg = seg_ref[...]
    n_keys = BLK + 1
    for s in range(n_new):
        a, r = s // dil, s % dil
        k_s = slot[cur, 0, pl.ds(a, n_keys), r]
        v_s = slot[cur, 1, pl.ds(a, n_keys), r]
        prod = (k_s * q_ref[s][None]).astype(BF16).reshape(n_keys * hkv, HEAD_DIM)
        sc = jnp.dot(prod, seg, preferred_element_type=F32).reshape(n_keys, hkv, HEAD_DIM)
        m = jnp.max(sc, axis=0)
        p = jnp.exp(sc - m[None])
        l = jnp.sum(p, axis=0)
        o = jnp.sum(p * v_s, axis=0)
        if mode == "merge":
            out_ref[s] = _merge3(o, m, l, on1_ref[s], ls1_ref[s], on2_ref[s], ls2_ref[s])
        else:
            on_ref[s] = o / l
            ls_ref[s] = m + jnp.log(l)

    for c in out_copies(n, cur):
        c.start()

    @pl.when(n == n_steps - 1)
    def _():
        for c in out_copies(n, cur):
            c.wait()


def window_attn(q, kn, vn, cache, layer, *, dil, mode, seg, prev_cache=None, extras=()):
    n_layers, ns, _, win, hkv, hd = cache.shape
    n_new = q.shape[0] // ns
    assert win == dil * BLK and (n_new % dil == 0 or dil == 2 * n_new)
    wd = win // dil
    cache_v = cache.reshape(n_layers, ns, 2, wd, dil, hkv, hd)
    tok = pl.BlockSpec((n_new, hkv, hd), lambda i: (i, 0, 0))
    in_specs = [tok, tok, tok, _const_spec(seg.shape)]
    args = [q, kn, vn, seg]
    if mode == "merge":
        in_specs += [tok] * 4
        args += list(extras)
    in_specs.append(pl.BlockSpec(memory_space=pl.ANY))
    args.append(cache_v)
    aliases = {}
    if prev_cache is not None:
        in_specs.append(pl.BlockSpec(memory_space=pl.ANY))
        aliases[len(args)] = 0
        args.append(prev_cache.reshape(cache_v.shape))
    n_tok_out = 2 if mode == "lse" else 1
    tok_shape = jax.ShapeDtypeStruct(q.shape, F32)
    res = pl.pallas_call(
        functools.partial(_window_attn_kernel, layer=layer, win=win, dil=dil, hkv=hkv, mode=mode, n_new=n_new),
        grid=(ns,),
        in_specs=in_specs,
        out_specs=[pl.BlockSpec(memory_space=pl.ANY)] + [tok] * n_tok_out,
        out_shape=[jax.ShapeDtypeStruct(cache_v.shape, F32)] + [tok_shape] * n_tok_out,
        scratch_shapes=[pltpu.VMEM((2, 2, wd + -(-n_new // dil), dil, hkv, hd), F32),
                        pltpu.SemaphoreType.DMA((2,)), pltpu.SemaphoreType.DMA((2,))],
        input_output_aliases=aliases,
        compiler_params=pltpu.CompilerParams(dimension_semantics=("arbitrary",), vmem_limit_bytes=VMEM_LIMIT),
        name=f"window_attn_d{dil}_{mode}",
    )(*args)
    return res[0].reshape(cache.shape), tuple(res[1:])


def _s5_kernel(*refs, tm, n_cplx, per_block):
    (u_ref, b0_ref, b1_ref, c0_ref, c1_ref, are_ref, aim_ref, d_ref, wglu_ref, bglu_ref) = refs[:10]
    if per_block:
        h0_ref, ob_ref, st_ref, hs_ref = refs[10:]
    else:
        ob_ref, st_ref, hs_ref, carry_ref = refs[10:]

        @pl.when(pl.program_id(1) == 0)
        def _():
            carry_ref[...] = jnp.zeros_like(carry_ref)

    half = n_cplx // 2
    u = u_ref[...]
    ub = u.astype(BF16)
    dh = u.shape[1] // 2
    hs_ref[:, 0:2 * half] = jnp.dot(ub[:, :dh], b0_ref[...], preferred_element_type=F32)
    hs_ref[:, 2 * half:4 * half] = jnp.dot(ub[:, dh:], b1_ref[...], preferred_element_type=F32)

    n_blk = tm // SUBLANES
    cols_per_loop = 2
    for c0 in range(0, n_cplx // LANES, cols_per_loop):
        consts, lanes = [], []
        for c in range(c0, c0 + cols_per_loop):
            sl = slice(c * LANES, (c + 1) * LANES)
            ar, ai = are_ref[:, sl], aim_ref[:, sl]
            consts.append([(ar[k * 8:(k + 1) * 8], ai[k * 8:(k + 1) * 8]) for k in range(4)])
            base = (c * LANES // half) * 2 * half + (c * LANES) % half
            lanes.append((slice(base, base + LANES), slice(base + half, base + half + LANES)))

        def scan8(k, ci, cr, cim, consts=consts, lanes=lanes):
            rows = pl.ds(pl.multiple_of(k * SUBLANES, SUBLANES), SUBLANES)
            lr, li = lanes[ci]
            xr, xi = hs_ref[rows, lr], hs_ref[rows, li]
            for step, sh in enumerate((1, 2, 4)):
                ar, ai = consts[ci][step]
                sr, si = pltpu.roll(xr, sh, 0), pltpu.roll(xi, sh, 0)
                xr, xi = xr + ar * sr - ai * si, xi + ar * si + ai * sr
            pr, pi = consts[ci][3]
            xr, xi = xr + pr * cr - pi * cim, xi + pr * cim + pi * cr
            hs_ref[rows, lr] = xr
            hs_ref[rows, li] = xi
            return xr[SUBLANES - 1:SUBLANES], xi[SUBLANES - 1:SUBLANES]

        def blk(k, carry):
            return tuple(scan8(k, ci, *carry[ci]) for ci in range(cols_per_loop))

        def blk_own_state(kk, carry, lanes=lanes):
            grp = pl.ds(pl.multiple_of(kk * SUBLANES, SUBLANES), SUBLANES)
            for ci in range(cols_per_loop):
                h0r, h0i = h0_ref[grp, lanes[ci][0]], h0_ref[grp, lanes[ci][1]]
                last = [scan8(kk * SUBLANES + jj, ci, h0r[jj:jj + 1], h0i[jj:jj + 1]) for jj in range(SUBLANES)]
                st_ref[grp, lanes[ci][0]] = jnp.concatenate([t[0] for t in last], axis=0)
                st_ref[grp, lanes[ci][1]] = jnp.concatenate([t[1] for t in last], axis=0)
            return carry

        if per_block:
            lax.fori_loop(0, n_blk // SUBLANES, blk_own_state, 0)
        else:
            init = tuple((carry_ref[:, lanes[ci][0]], carry_ref[:, lanes[ci][1]]) for ci in range(cols_per_loop))
            fin = lax.fori_loop(0, n_blk, blk, init, unroll=2)
            for ci in range(cols_per_loop):
                carry_ref[:, lanes[ci][0]] = fin[ci][0]
                carry_ref[:, lanes[ci][1]] = fin[ci][1]

    if not per_block:
        st_ref[...] = carry_ref[...]

    y_lo = jnp.dot(hs_ref[:, 0:2 * half].astype(BF16), c0_ref[...], preferred_element_type=F32)
    y_hi = jnp.dot(hs_ref[:, 2 * half:4 * half].astype(BF16), c1_ref[...], preferred_element_type=F32)
    y = jnp.concatenate([y_lo, y_hi], axis=1) + d_ref[...] * u
    zb = jax.nn.gelu(y)
    gate = jnp.dot(zb.astype(BF16), wglu_ref[...], preferred_element_type=F32) + bglu_ref[...]
    ob_ref[...] = zb * jax.nn.sigmoid(gate)


def s5_layer(z, sp, *, uoff, row0, n_seq, seq, h0=None, merged_into=None):
    m_rows, _ = z.shape
    db = sp["d"].shape[1]
    n_cplx = sp["a_re"].shape[1]
    per_block = h0 is not None
    cu = uoff // db
    consts = [sp["b0"], sp["b1"], sp["c0"], sp["c1"], sp["a_re"], sp["a_im"], sp["d"], sp["w_glu"], sp["b_glu"]]
    if per_block:
        assert seq == SUBLANES
        rows = n_seq * seq
        tm = _pick(rows, (256, 128, 64))
        rb0 = row0 // tm
        grid = (rows // tm,)
        in_specs = [pl.BlockSpec((tm, db), lambda i: (rb0 + i, cu))] + [_const_spec(c.shape) for c in consts]
        in_specs.append(pl.BlockSpec((tm // SUBLANES, 2 * n_cplx), lambda i: (i, 0)))
        args = [z] + consts + [h0]
        out_specs = [pl.BlockSpec((tm, db), lambda i: (rb0 + i, 0)),
                     pl.BlockSpec((tm // SUBLANES, 2 * n_cplx), lambda i: (i, 0))]
        st_shape = jax.ShapeDtypeStruct((n_seq, 2 * n_cplx), F32)
        scratch = [pltpu.VMEM((tm, 2 * n_cplx), F32)]
        sem = ("arbitrary",)
    else:
        tm = _pick(seq, (256, 128))
        nt = seq // tm
        rb0 = row0 // tm
        grid = (n_seq, nt)
        in_specs = [pl.BlockSpec((tm, db), lambda n, t: (rb0 + n * nt + t, cu))] + [_const_spec(c.shape) for c in consts]
        args = [z] + consts
        out_specs = [pl.BlockSpec((tm, db), lambda n, t: (rb0 + n * nt + t, 0)),
                     pl.BlockSpec((None, 1, 2 * n_cplx), lambda n, t: (n, 0, 0))]
        st_shape = jax.ShapeDtypeStruct((n_seq, 1, 2 * n_cplx), F32)
        scratch = [pltpu.VMEM((tm, 2 * n_cplx), F32), pltpu.VMEM((1, 2 * n_cplx), F32)]
        sem = ("arbitrary", "arbitrary")
    aliases = {}
    if merged_into is not None:
        in_specs.append(pl.BlockSpec(memory_space=pl.ANY))
        aliases[len(args)] = 0
        args.append(merged_into)
    ob, st = pl.pallas_call(
        functools.partial(_s5_kernel_entry, tm=tm, n_cplx=n_cplx, per_block=per_block,
                          n_alias=len(aliases)),
        grid=grid, in_specs=in_specs, out_specs=out_specs,
        out_shape=[jax.ShapeDtypeStruct((m_rows, db), F32), st_shape],
        scratch_shapes=scratch, input_output_aliases=aliases,
        compiler_params=pltpu.CompilerParams(dimension_semantics=sem, vmem_limit_bytes=VMEM_LIMIT),
        name="s5_sample" if per_block else "s5_prompt",
    )(*args)
    return ob, st.reshape(n_seq, 2 * n_cplx)


def _s5_kernel_entry(*refs, tm, n_cplx, per_block, n_alias):
    n_in = 11 if per_block else 10
    refs = refs[:n_in] + refs[n_in + n_alias:]
    _s5_kernel(*refs, tm=tm, n_cplx=n_cplx, per_block=per_block)


def _cmul(a, b):
    return a[0] * b[0] - a[1] * b[1], a[0] * b[1] + a[1] * b[0]


def s5_params(a_re, a_im, log_dt, b_re, b_im, c_re, c_im, d_skip, w_glu, b_glu):
    g_b, n_s = a_re.shape
    gh = g_b // 2
    hp = lax.Precision.HIGHEST
    lam = lax.complex(a_re.astype(F32), a_im.astype(F32))
    dt = jnp.exp(log_dt.astype(F32))[:, None]
    abar = jnp.exp(lam * dt)
    bbar = ((abar - 1.0) / lam)[..., None] * lax.complex(b_re.astype(F32), b_im.astype(F32))
    eye = jnp.eye(gh, dtype=F32)

    def in_map(half):
        bb = bbar[half * gh:(half + 1) * gh]
        blocks = [jnp.einsum("gsc,gh->gchs", part, eye, precision=hp).reshape(gh * GROUP_CH, gh * n_s)
                  for part in (bb.real, bb.imag)]
        return jnp.concatenate(blocks, axis=1).astype(BF16)

    def out_map(half):
        blocks = [jnp.einsum("gcs,gh->gshc", part[half * gh:(half + 1) * gh], eye,
                             precision=hp).reshape(gh * n_s, gh * GROUP_CH)
                  for part in (c_re.astype(F32), -c_im.astype(F32))]
        return jnp.concatenate(blocks, axis=0).astype(BF16)

    a1 = (abar.real.reshape(1, g_b * n_s), abar.imag.reshape(1, g_b * n_s))
    pw = [a1]
    for _ in range(SUBLANES - 1):
        pw.append(_cmul(pw[-1], a1))
    row = jnp.arange(SUBLANES)[:, None]
    tabs_re, tabs_im = [], []
    for sh in (1, 2, 4):
        tabs_re.append(jnp.where(row >= sh, pw[sh - 1][0], 0.0))
        tabs_im.append(jnp.where(row >= sh, pw[sh - 1][1], 0.0))
    tabs_re.append(jnp.concatenate([p[0] for p in pw], axis=0))
    tabs_im.append(jnp.concatenate([p[1] for p in pw], axis=0))
    return dict(b0=in_map(0), b1=in_map(1), c0=out_map(0), c1=out_map(1),
                a_re=jnp.concatenate(tabs_re, axis=0), a_im=jnp.concatenate(tabs_im, axis=0),
                d=d_skip.astype(F32).reshape(1, -1), w_glu=w_glu.astype(BF16),
                b_glu=b_glu.astype(F32).reshape(1, -1))


def _state_to_lanes(re, im):
    n, g_b, n_s = re.shape
    gh = g_b // 2
    parts = []
    for half in range(2):
        for x in (re, im):
            parts.append(x[:, half * gh:(half + 1) * gh].reshape(n, gh * n_s))
    return jnp.concatenate(parts, axis=1).astype(F32)


def _lanes_to_state(st, g_b, n_s):
    n = st.shape[0]
    gh = g_b // 2
    st = st.reshape(n, 2, 2, gh, n_s)
    re = st[:, :, 0].reshape(n, g_b, n_s)
    im = st[:, :, 1].reshape(n, g_b, n_s)
    return re, im


def _post_kernel(*refs, n_a, ff_chunk):
    x_ref = refs[0]
    a_refs = refs[1:1 + n_a]
    p_ref = refs[1 + n_a]
    wo_refs = refs[2 + n_a:2 + 2 * n_a]
    (g_pm, g_pf, w1_ref, w2_ref, g_qf, g_ple, wg_ref, wp_ref, o_ref) = refs[2 + 2 * n_a:]

    x = x_ref[...]
    mix = None
    for a_ref, w_ref in zip(a_refs, wo_refs):
        t = jnp.dot(a_ref[...].astype(BF16), w_ref[...], preferred_element_type=F32)
        mix = t if mix is None else mix + t
    x = x + _rms(mix, g_pm[...])
    hf = _rms(x, g_pf[...]).astype(BF16)
    d_ff = w1_ref.shape[1]
    f = None
    for c in range(0, d_ff, ff_chunk):
        t = jnp.dot(hf, w1_ref[:, c:c + ff_chunk], preferred_element_type=F32)
        t = jnp.square(jnp.maximum(t, 0.0)).astype(BF16)
        t = jnp.dot(t, w2_ref[c:c + ff_chunk, :], preferred_element_type=F32)
        f = t if f is None else f + t
    x = x + _rms(f, g_qf[...])
    gate = jax.nn.sigmoid(jnp.dot(_rms(x, g_ple[...]).astype(BF16), wg_ref[...], preferred_element_type=F32))
    pe = jnp.dot(p_ref[...].astype(BF16), wp_ref[...], preferred_element_type=F32)
    o_ref[...] = x + pe * gate


def post_block(x, a_list, wo_list, p_all, layer, g_pm, g_pf, w1, w2, g_qf, g_ple, wg, wp):
    m, d = x.shape
    tm = _pick(m, (512, 256, 128))
    n_a = len(a_list)
    consts = list(wo_list) + [g_pm, g_pf, w1, w2, g_qf, g_ple, wg, wp]
    in_specs = [pl.BlockSpec((tm, d), lambda i: (i, 0))]
    in_specs += [pl.BlockSpec((tm, a.shape[1]), lambda i: (i, 0)) for a in a_list]
    in_specs += [pl.BlockSpec((None, tm, p_all.shape[2]), lambda i: (layer, i, 0))]
    in_specs += [pl.BlockSpec(c.shape, lambda i, nd=c.ndim: (0,) * nd, pipeline_mode=pl.Buffered(1)) for c in consts]
    return pl.pallas_call(
        functools.partial(_post_kernel, n_a=n_a, ff_chunk=min(1024, w1.shape[1])),
        grid=(m // tm,),
        in_specs=in_specs,
        out_specs=pl.BlockSpec((tm, d), lambda i: (i, 0)),
        out_shape=jax.ShapeDtypeStruct((m, d), F32),
        compiler_params=pltpu.CompilerParams(dimension_semantics=("parallel",), vmem_limit_bytes=VMEM_LIMIT),
        name="post_block",
    )(x, *a_list, p_all, *consts)


def _seg_matrix(width):
    h = jnp.arange(width) // HEAD_DIM
    return jnp.where(h[:, None] == h[None, :], HEAD_DIM ** -0.5, 0.0).astype(BF16)


def kernel(x_prompt, x_sample, cache_a1_kv, cache_a2_kv, cache_a3_kv, state_b_re, state_b_im, cache_c_kv, p_prompt, p_sample, w_in_ab, w_out_ab, ssm_a_re, ssm_a_im, ssm_log_dt, ssm_b_re, ssm_b_im, ssm_c_re, ssm_c_im, ssm_d, w_glu, b_glu, w_in_c, sinks_c, w_out_c, g_pre_mix, g_post_mix, g_pre_ffn, g_post_ffn, g_ple, w_ff1, w_ff2, w_ple, w_ple_gate):
    batch, seq, d_model = x_prompt.shape
    ns, n_new, _ = x_sample.shape
    depth = g_pre_mix.shape[0]
    n_pt = batch * seq
    n_st = ns * n_new
    h_a = d_model // (2 * HEAD_DIM)
    d_a = h_a * HEAD_DIM
    n_dil = len(DILATED)
    qkv_a = n_dil * 3 * d_a
    g_b, n_s = ssm_a_re.shape[1:]
    h_c = d_model // HEAD_DIM
    kv_c = cache_c_kv.shape[4]
    rep_c = h_c // kv_c
    wk_c = kv_c * HEAD_DIM

    x = jnp.concatenate([x_prompt.reshape(n_pt, d_model), x_sample.reshape(n_st, d_model)], axis=0)
    p_all = jnp.concatenate([p_prompt.reshape(depth, n_pt, -1), p_sample.reshape(depth, n_st, -1)], axis=1)

    caches_a = [cache_a1_kv, cache_a2_kv, cache_a3_kv]
    cache_c = cache_c_kv.reshape(cache_c_kv.shape[:4] + (wk_c,))
    seg_c = _seg_matrix(LANES)
    seg_a = _seg_matrix(HEAD_DIM)
    nb_seq_c = _pick(ns, (8, 4, 2, 1))

    perm = jnp.arange(h_c * HEAD_DIM).reshape(kv_c, rep_c, HEAD_DIM).transpose(1, 0, 2).reshape(-1)

    def row_g(v):
        return v.astype(F32).reshape(1, -1)

    new_a = [None] * n_dil
    new_c = None
    a_prompt = [[] for _ in range(n_dil)]
    c_prompt = []
    st_prompt, st_sample = [], []

    for i in range(depth):
        j = i // 2
        if i % 2 == 0:
            z = norm_matmul(x, row_g(g_pre_mix[i]), w_in_ab[j].astype(BF16))
            for g, (win, dil) in enumerate(DILATED):
                keep = min(win, seq)
                kv_tail = [jnp.stack([z[(b + 1) * seq - keep:(b + 1) * seq, c0:c0 + d_a]
                                      for c0 in (g * 3 * d_a + d_a, g * 3 * d_a + 2 * d_a)]) for b in range(batch)]
                a_prompt[g].append(jnp.stack(kv_tail).reshape(batch, 2, keep, h_a, HEAD_DIM))
            zs = z[n_pt:, :qkv_a].reshape(n_st, n_dil, 3, h_a, HEAD_DIM)
            pairs, pairs_s = [], []
            for g in (2, 1, 0):
                win, dil = DILATED[g]
                off = g * 3 * d_a
                common = dict(dil=dil, qoff=off, koff=off + d_a, voff=off + 2 * d_a)
                if g > 0:
                    pairs += prompt_attn(z, batch=batch, seq=seq, hq=h_a, hkv=h_a, mode="lse", **common)
                    new_a[g], on_ls = window_attn(zs[:, g, 0], zs[:, g, 1], zs[:, g, 2], caches_a[g], j, dil=dil,
                                                  mode="lse", seg=seg_a, prev_cache=new_a[g])
                    pairs_s += on_ls
                else:
                    out_a = prompt_attn(z, batch=batch, seq=seq, hq=h_a, hkv=h_a, mode="merge", extras=pairs,
                                        **common)
                    new_a[g], (out_s,) = window_attn(zs[:, g, 0], zs[:, g, 1], zs[:, g, 2], caches_a[g], j, dil=dil,
                                                     mode="merge", seg=seg_a, prev_cache=new_a[g], extras=pairs_s)
                    out_a = lax.dynamic_update_slice(out_a, out_s.reshape(n_st, d_a), (n_pt, 0))
            sp = s5_params(ssm_a_re[j], ssm_a_im[j], ssm_log_dt[j], ssm_b_re[j], ssm_b_im[j], ssm_c_re[j],
                           ssm_c_im[j], ssm_d[j], w_glu[j], b_glu[j])
            out_b, st_p = s5_layer(z, sp, uoff=qkv_a, row0=0, n_seq=batch, seq=seq)
            out_b, st_s = s5_layer(z, sp, uoff=qkv_a, row0=n_pt, n_seq=ns, seq=n_new,
                                   h0=_state_to_lanes(state_b_re[j], state_b_im[j]), merged_into=out_b)
            st_prompt.append(st_p)
            st_sample.append(st_s)
            wo = w_out_ab[j].astype(BF16)
            a_list, wo_list = [out_a, out_b], [wo[:d_a], wo[d_a:]]
        else:
            w_in = w_in_c[j]
            w_in = jnp.concatenate([w_in[:, :h_c * HEAD_DIM][:, perm], w_in[:, h_c * HEAD_DIM:]], axis=1)
            z = norm_matmul(x, row_g(g_pre_mix[i]), w_in.astype(BF16))
            qw = h_c * HEAD_DIM
            keep = min(WIN_C, seq)
            kv_tail = [jnp.stack([z[(b + 1) * seq - keep:(b + 1) * seq, c0:c0 + wk_c] for c0 in (qw, qw + wk_c)])
                       for b in range(batch)]
            c_prompt.append(jnp.stack(kv_tail).reshape(batch, 2, keep, kv_c, HEAD_DIM))
            sink = jnp.broadcast_to(sinks_c[j].astype(F32).reshape(kv_c, rep_c, 1).transpose(1, 0, 2),
                                    (rep_c, kv_c, HEAD_DIM)).reshape(1, qw)
            common = dict(dil=1, qoff=0, koff=qw, voff=qw + wk_c, mode="sink", sink=sink)
            att = prompt_attn(z, batch=batch, seq=seq, hq=h_c, hkv=kv_c, **common)
            new_c, (att,) = sample_attn(z, cache_c, j, row0=n_pt, n_new=n_new, hkv=kv_c, rep=rep_c, seg=seg_c,
                                        nb_seq=nb_seq_c, prev_cache=new_c, merged_into=(att,), **common)
            a_list, wo_list = [att], [w_out_c[j][perm].astype(BF16)]
        x = post_block(x, a_list, wo_list, p_all, i, row_g(g_post_mix[i]), row_g(g_pre_ffn[i]),
                       w_ff1[i].astype(BF16), w_ff2[i].astype(BF16), row_g(g_post_ffn[i]), row_g(g_ple[i]),
                       w_ple_gate[i].astype(BF16), w_ple[i].astype(BF16))

    y_prompt = x[:n_pt].reshape(batch, seq, d_model)
    y_sample = x[n_pt:].reshape(ns, n_new, d_model)
    outs_a = []
    for g in range(n_dil):
        outs_a.append(jnp.stack(a_prompt[g]))
        outs_a.append(new_a[g])
    sp_re, sp_im = zip(*[_lanes_to_state(s, g_b, n_s) for s in st_prompt])
    ss_re, ss_im = zip(*[_lanes_to_state(s, g_b, n_s) for s in st_sample])
    return (y_prompt, y_sample, *outs_a,
            jnp.stack(sp_re), jnp.stack(ss_re), jnp.stack(sp_im), jnp.stack(ss_im),
            jnp.stack(c_prompt), new_c.reshape(new_c.shape[:4] + (kv_c, HEAD_DIM)))
```

```python
import functools

import jax
import jax.numpy as jnp
from jax import lax
from jax.experimental import pallas as pl
from jax.experimental.pallas import tpu as pltpu

F32 = jnp.float32
BF16 = jnp.bfloat16

HEAD_DIM = 64
BLK = 128
DILATED = ((128, 1), (512, 4), (2048, 16))
GROUP_CH = 16
N_STATE = 64
WIN_C = 128
RMS_EPS = 1e-6
NEG_INF = -1e30
SUBLANES = 8
LANES = 128
VMEM_LIMIT = 48 * 1024 * 1024
WINDOW_VMEM_LIMIT = 56 * 1024 * 1024


def _pick(n, prefs):
    for p in prefs:
        if n % p == 0:
            return p
    raise ValueError(f"no tile in {prefs} divides {n}")


def _rms(x, g):
    ms = jnp.mean(x * x, axis=-1, keepdims=True)
    return x * lax.rsqrt(ms + RMS_EPS) * g


def _const_spec(shape):
    nd = len(shape)
    return pl.BlockSpec(shape, lambda *_: (0,) * nd)


def _norm_matmul_kernel(x_ref, g_ref, w_ref, o_ref, h_ref):
    @pl.when(pl.program_id(1) == 0)
    def _():
        h_ref[...] = _rms(x_ref[...], g_ref[...]).astype(BF16)

    o_ref[...] = jnp.dot(h_ref[...], w_ref[...], preferred_element_type=F32)


def norm_matmul(x, g, w):
    m, d = x.shape
    n = w.shape[1]
    tm = _pick(m, (1024, 512, 256, 128))
    tn = _pick(n, (512, 256, 128))
    return pl.pallas_call(
        _norm_matmul_kernel,
        grid=(m // tm, n // tn),
        in_specs=[pl.BlockSpec((tm, d), lambda i, j: (i, 0)),
                  pl.BlockSpec((1, d), lambda i, j: (0, 0)),
                  pl.BlockSpec((d, tn), lambda i, j: (0, j))],
        out_specs=pl.BlockSpec((tm, tn), lambda i, j: (i, j)),
        out_shape=jax.ShapeDtypeStruct((m, n), F32),
        scratch_shapes=[pltpu.VMEM((tm, d), BF16)],
        compiler_params=pltpu.CompilerParams(dimension_semantics=("parallel", "arbitrary"),
                                             vmem_limit_bytes=VMEM_LIMIT),
        name="norm_matmul",
    )(x, g, w)


def _merge3(o, m, l, on1, ls1, on2, ls2):
    ls0 = m + jnp.log(l)
    mx = jnp.maximum(jnp.maximum(ls0, ls1), ls2)
    w0 = jnp.exp(ls0 - mx)
    w1 = jnp.exp(ls1 - mx)
    w2 = jnp.exp(ls2 - mx)
    return (w0 * (o / l) + w1 * on1 + w2 * on2) / (w0 + w1 + w2)


def _with_sink(o, m, l, sk):
    mx = jnp.maximum(m, sk)
    sc = jnp.exp(m - mx)
    return o * sc / (l * sc + jnp.exp(sk - mx))


def _prompt_attn_kernel(*refs, hq, hkv, mode):
    q_ref, kp_ref, kc_ref, vp_ref, vc_ref = refs[:5]
    rest = refs[5:]
    if mode == "sink":
        sink_ref, out_ref = rest
    elif mode == "merge":
        on1_ref, ls1_ref, on2_ref, ls2_ref, out_ref = rest
    else:
        on_ref, ls_ref = rest

    b = pl.program_id(2)
    qi = lax.broadcasted_iota(jnp.int32, (BLK, 2 * BLK), 0)
    ki = lax.broadcasted_iota(jnp.int32, (BLK, 2 * BLK), 1)
    valid = (ki >= qi) & (ki <= qi + BLK) & ((ki >= BLK) | (b > 0))

    kv = {}
    for h in range(hq):
        g = h % hkv
        if g not in kv:
            sl = slice(g * HEAD_DIM, (g + 1) * HEAD_DIM)
            k2 = jnp.concatenate([kp_ref[:, sl], kc_ref[:, sl]], axis=0).astype(BF16)
            v2 = jnp.concatenate([vp_ref[:, sl], vc_ref[:, sl]], axis=0).astype(BF16)
            kv[g] = (k2, v2)
        k2, v2 = kv[g]
        hs = slice(h * HEAD_DIM, (h + 1) * HEAD_DIM)
        qh = q_ref[:, hs].astype(BF16)
        s = lax.dot_general(qh, k2, (((1,), (1,)), ((), ())), preferred_element_type=F32)
        s = jnp.where(valid, s * (HEAD_DIM ** -0.5), NEG_INF)
        m = jnp.max(s, axis=-1, keepdims=True)
        p = jnp.exp(s - m)
        l = jnp.sum(p, axis=-1, keepdims=True)
        o = jnp.dot(p.astype(BF16), v2, preferred_element_type=F32)
        if mode == "sink":
            out_ref[:, hs] = _with_sink(o, m, l, sink_ref[:, hs])
        elif mode == "merge":
            out_ref[:, hs] = _merge3(o, m, l, on1_ref[:, hs], ls1_ref[:, hs], on2_ref[:, hs], ls2_ref[:, hs])
        else:
            on_ref[:, hs] = o / l
            ls_ref[:, hs] = jnp.broadcast_to(m + jnp.log(l), (BLK, HEAD_DIM))


def prompt_attn(z, *, batch, seq, dil, qoff, koff, voff, hq, hkv, mode, extras=(), sink=None):
    m_rows, nz = z.shape
    wq, wk = hq * HEAD_DIM, hkv * HEAD_DIM
    nbk = seq // dil // BLK
    zv = z.reshape(m_rows // dil, dil * nz)
    cq, ck, cv = qoff // wq, koff // wk, voff // wk

    def row(n, b):
        return n * nbk + b

    in_specs = [
        pl.BlockSpec((BLK, wq), lambda n, r, b: (row(n, b), r * (nz // wq) + cq)),
        pl.BlockSpec((BLK, wk), lambda n, r, b: (row(n, jnp.maximum(b - 1, 0)), r * (nz // wk) + ck)),
        pl.BlockSpec((BLK, wk), lambda n, r, b: (row(n, b), r * (nz // wk) + ck)),
        pl.BlockSpec((BLK, wk), lambda n, r, b: (row(n, jnp.maximum(b - 1, 0)), r * (nz // wk) + cv)),
        pl.BlockSpec((BLK, wk), lambda n, r, b: (row(n, b), r * (nz // wk) + cv)),
    ]
    args = [zv, zv, zv, zv, zv]
    out_block = pl.BlockSpec((BLK, wq), lambda n, r, b: (row(n, b), r))
    out_one = jax.ShapeDtypeStruct((m_rows // dil, dil * wq), F32)
    if mode == "sink":
        in_specs.append(pl.BlockSpec((1, wq), lambda n, r, b: (0, 0)))
        args.append(sink)
        out_specs, out_shape = out_block, out_one
    elif mode == "merge":
        assert dil == 1
        for e in extras:
            in_specs.append(pl.BlockSpec((BLK, wq), lambda n, r, b: (row(n, b), 0)))
            args.append(e)
        out_specs, out_shape = out_block, out_one
    else:
        out_specs, out_shape = [out_block, out_block], [out_one, out_one]

    res = pl.pallas_call(
        functools.partial(_prompt_attn_kernel, hq=hq, hkv=hkv, mode=mode),
        grid=(batch, dil, nbk),
        in_specs=in_specs, out_specs=out_specs, out_shape=out_shape,
        compiler_params=pltpu.CompilerParams(dimension_semantics=("parallel", "parallel", "arbitrary"),
                                             vmem_limit_bytes=VMEM_LIMIT),
        name=f"prompt_attn_d{dil}_{mode}",
    )(*args)
    if mode in ("sink", "merge"):
        return res.reshape(m_rows, wq)
    return res[0].reshape(m_rows, wq), res[1].reshape(m_rows, wq)


def _sample_attn_kernel(*refs, win, dil, hq, hkv, mode, n_alias, n_new):
    wk = hkv * HEAD_DIM
    q_ref, kn_ref, vn_ref, knt_ref, vnt_ref, cache_ref = refs[:6]
    pos = 6
    if mode == "sink":
        sink_ref = refs[pos]
        pos += 1
    elif mode == "merge":
        on1_ref, ls1_ref, on2_ref, ls2_ref = refs[pos:pos + 4]
        pos += 4
    pos += n_alias
    co_ref = refs[pos]
    pos += 1
    if mode == "lse":
        on_ref, ls_ref = refs[pos:pos + 2]
    else:
        out_ref = refs[pos]

    for c, new_t in ((0, knt_ref), (1, vnt_ref)):
        co_ref[c, :, 0:win - n_new] = cache_ref[c, :, n_new:win]
        co_ref[c, :, win - n_new:win] = new_t[...]

    rows = hq * n_new
    reps = hq // hkv
    q = q_ref[...] * (HEAD_DIM ** -0.5)
    lane_head = lax.broadcasted_iota(jnp.int32, (hkv * n_new, wk), 1) // HEAD_DIM
    row_head = lax.broadcasted_iota(jnp.int32, (hkv * n_new, wk), 0) // n_new
    q_blk = jnp.concatenate(
        [jnp.where(lane_head == row_head, jnp.concatenate([q[:, r * wk:(r + 1) * wk]] * hkv, axis=0), 0.0)
         for r in range(reps)], axis=0).astype(BF16)

    s_c = jnp.dot(q_blk, cache_ref[0].astype(BF16), preferred_element_type=F32)
    s_n = lax.dot_general(q_blk, kn_ref[...].astype(BF16), (((1,), (1,)), ((), ())),
                          preferred_element_type=F32)
    tok_c = lax.broadcasted_iota(jnp.int32, (rows, win), 0) % n_new
    pos_c = lax.broadcasted_iota(jnp.int32, (rows, win), 1)
    ok_c = (pos_c >= tok_c) & (((win + tok_c - pos_c) & (dil - 1)) == 0)
    tok_n = lax.broadcasted_iota(jnp.int32, (rows, n_new), 0) % n_new
    pos_n = lax.broadcasted_iota(jnp.int32, (rows, n_new), 1)
    ok_n = (pos_n <= tok_n) & (((tok_n - pos_n) & (dil - 1)) == 0)
    s_c = jnp.where(ok_c, s_c, NEG_INF)
    s_n = jnp.where(ok_n, s_n, NEG_INF)
    m = jnp.maximum(jnp.max(s_c, axis=-1, keepdims=True), jnp.max(s_n, axis=-1, keepdims=True))
    p_c = jnp.exp(s_c - m)
    p_n = jnp.exp(s_n - m)
    l = jnp.sum(p_c, axis=-1, keepdims=True) + jnp.sum(p_n, axis=-1, keepdims=True)
    o = lax.dot_general(p_c.astype(BF16), cache_ref[1].astype(BF16), (((1,), (1,)), ((), ())),
                        preferred_element_type=F32)
    o = o + jnp.dot(p_n.astype(BF16), vn_ref[...].astype(BF16), preferred_element_type=F32)

    for h in range(hq):
        g = h % hkv
        rs = slice(h * n_new, (h + 1) * n_new)
        hs = slice(h * HEAD_DIM, (h + 1) * HEAD_DIM)
        o_h, m_h, l_h = o[rs, g * HEAD_DIM:(g + 1) * HEAD_DIM], m[rs], l[rs]
        if mode == "sink":
            out_ref[:, hs] = _with_sink(o_h, m_h, l_h, sink_ref[:, hs])
        elif mode == "merge":
            out_ref[:, hs] = _merge3(o_h, m_h, l_h, on1_ref[:, hs], ls1_ref[:, hs], on2_ref[:, hs], ls2_ref[:, hs])
        else:
            on_ref[:, hs] = o_h / l_h
            ls_ref[:, hs] = jnp.broadcast_to(m_h + jnp.log(l_h), (n_new, HEAD_DIM))


def sample_attn(z, cache_t, layer, *, row0, n_new, dil, qoff, koff, voff, hq, hkv, mode,
                prev_cache=None, extras=(), sink=None, merged_into=()):
    m_rows, nz = z.shape
    n_layers, ns, _, wk, win = cache_t.shape
    wq = hq * HEAD_DIM
    assert win == dil * BLK and row0 % n_new == 0 and dil & (dil - 1) == 0
    rb0 = row0 // n_new
    cq, ck, cv = qoff // wq, koff // wk, voff // wk
    znew = z[row0:row0 + ns * n_new]
    knt = znew[:, koff:koff + wk].reshape(ns, n_new, wk).transpose(0, 2, 1)
    vnt = znew[:, voff:voff + wk].reshape(ns, n_new, wk).transpose(0, 2, 1)

    new_t = pl.BlockSpec((None, wk, n_new), lambda i: (i, 0, 0))
    win_block = pl.BlockSpec((None, None, 2, wk, win), lambda i: (layer, i, 0, 0, 0))
    in_specs = [
        pl.BlockSpec((n_new, wq), lambda i: (rb0 + i, cq)),
        pl.BlockSpec((n_new, wk), lambda i: (rb0 + i, ck)),
        pl.BlockSpec((n_new, wk), lambda i: (rb0 + i, cv)),
        new_t, new_t, win_block,
    ]
    args = [z, z, z, knt, vnt, cache_t]
    if mode == "sink":
        in_specs.append(_const_spec(sink.shape))
        args.append(sink)
    elif mode == "merge":
        for e in extras:
            in_specs.append(pl.BlockSpec((n_new, wq), lambda i: (rb0 + i, 0)))
            args.append(e)
    aliases = {}
    alias_args = ([prev_cache] if prev_cache is not None else []) + list(merged_into)
    first_out = 0 if prev_cache is not None else 1
    for a_i, a in enumerate(alias_args):
        in_specs.append(pl.BlockSpec(memory_space=pl.ANY))
        aliases[len(args)] = first_out + a_i
        args.append(a)

    tok_block = pl.BlockSpec((n_new, wq), lambda i: (rb0 + i, 0))
    tok_shape = jax.ShapeDtypeStruct((m_rows, wq), F32)
    n_tok_out = 2 if mode == "lse" else 1
    assert len(merged_into) == n_tok_out
    res = pl.pallas_call(
        functools.partial(_sample_attn_kernel, win=win, dil=dil, hq=hq, hkv=hkv, mode=mode,
                          n_alias=len(alias_args), n_new=n_new),
        grid=(ns,),
        in_specs=in_specs,
        out_specs=[win_block] + [tok_block] * n_tok_out,
        out_shape=[jax.ShapeDtypeStruct(cache_t.shape, F32)] + [tok_shape] * n_tok_out,
        input_output_aliases=aliases,
        compiler_params=pltpu.CompilerParams(dimension_semantics=("parallel",), vmem_limit_bytes=WINDOW_VMEM_LIMIT),
        name=f"sample_attn_d{dil}_{mode}",
    )(*args)
    return res[0], tuple(res[1:])


def _windows_feature_major(cache):
    n_l, ns, _, win, h, hd = cache.shape
    return cache.transpose(0, 1, 2, 4, 5, 3).reshape(n_l, ns, 2, h * hd, win)


def _windows_position_major(cache_t, heads):
    n_l, ns, _, wk, win = cache_t.shape
    return cache_t.reshape(n_l, ns, 2, heads, wk // heads, win).transpose(0, 1, 2, 5, 3, 4)


def _s5_kernel(*refs, tm, n_cplx, per_block):
    (u_ref, b0_ref, b1_ref, c0_ref, c1_ref, are_ref, aim_ref, d_ref, wglu_ref, bglu_ref) = refs[:10]
    if per_block:
        h0_ref, ob_ref, st_ref, hs_ref = refs[10:]
    else:
        ob_ref, st_ref, hs_ref, carry_ref = refs[10:]

        @pl.when(pl.program_id(1) == 0)
        def _():
            carry_ref[...] = jnp.zeros_like(carry_ref)

    half = n_cplx // 2
    u = u_ref[...]
    ub = u.astype(BF16)
    dh = u.shape[1] // 2
    hs_ref[:, 0:2 * half] = jnp.dot(ub[:, :dh], b0_ref[...], preferred_element_type=F32)
    hs_ref[:, 2 * half:4 * half] = jnp.dot(ub[:, dh:], b1_ref[...], preferred_element_type=F32)

    n_blk = tm // SUBLANES
    cols_per_loop = 2
    for c0 in range(0, n_cplx // LANES, cols_per_loop):
        consts, lanes = [], []
        for c in range(c0, c0 + cols_per_loop):
            sl = slice(c * LANES, (c + 1) * LANES)
            ar, ai = are_ref[:, sl], aim_ref[:, sl]
            consts.append([(ar[k * 8:(k + 1) * 8], ai[k * 8:(k + 1) * 8]) for k in range(4)])
            base = (c * LANES // half) * 2 * half + (c * LANES) % half
            lanes.append((slice(base, base + LANES), slice(base + half, base + half + LANES)))

        def scan8(k, ci, cr, cim, consts=consts, lanes=lanes):
            rows = pl.ds(pl.multiple_of(k * SUBLANES, SUBLANES), SUBLANES)
            lr, li = lanes[ci]
            xr, xi = hs_ref[rows, lr], hs_ref[rows, li]
            for step, sh in enumerate((1, 2, 4)):
                ar, ai = consts[ci][step]
                sr, si = pltpu.roll(xr, sh, 0), pltpu.roll(xi, sh, 0)
                xr, xi = xr + ar * sr - ai * si, xi + ar * si + ai * sr
            pr, pi = consts[ci][3]
            xr, xi = xr + pr * cr - pi * cim, xi + pr * cim + pi * cr
            hs_ref[rows, lr] = xr
            hs_ref[rows, li] = xi
            return xr[SUBLANES - 1:SUBLANES], xi[SUBLANES - 1:SUBLANES]

        def blk(k, carry):
            return tuple(scan8(k, ci, *carry[ci]) for ci in range(cols_per_loop))

        def blk_own_state(kk, carry, lanes=lanes):
            grp = pl.ds(pl.multiple_of(kk * SUBLANES, SUBLANES), SUBLANES)
            for ci in range(cols_per_loop):
                h0r, h0i = h0_ref[grp, lanes[ci][0]], h0_ref[grp, lanes[ci][1]]
                last = [scan8(kk * SUBLANES + jj, ci, h0r[jj:jj + 1], h0i[jj:jj + 1]) for jj in range(SUBLANES)]
                st_ref[grp, lanes[ci][0]] = jnp.concatenate([t[0] for t in last], axis=0)
                st_ref[grp, lanes[ci][1]] = jnp.concatenate([t[1] for t in last], axis=0)
            return carry

        if per_block:
            lax.fori_loop(0, n_blk // SUBLANES, blk_own_state, 0)
        else:
            init = tuple((carry_ref[:, lanes[ci][0]], carry_ref[:, lanes[ci][1]]) for ci in range(cols_per_loop))
            fin = lax.fori_loop(0, n_blk, blk, init, unroll=2)
            for ci in range(cols_per_loop):
                carry_ref[:, lanes[ci][0]] = fin[ci][0]
                carry_ref[:, lanes[ci][1]] = fin[ci][1]

    if not per_block:
        st_ref[...] = carry_ref[...]

    y_lo = jnp.dot(hs_ref[:, 0:2 * half].astype(BF16), c0_ref[...], preferred_element_type=F32)
    y_hi = jnp.dot(hs_ref[:, 2 * half:4 * half].astype(BF16), c1_ref[...], preferred_element_type=F32)
    y = jnp.concatenate([y_lo, y_hi], axis=1) + d_ref[...] * u
    zb = jax.nn.gelu(y)
    gate = jnp.dot(zb.astype(BF16), wglu_ref[...], preferred_element_type=F32) + bglu_ref[...]
    ob_ref[...] = zb * jax.nn.sigmoid(gate)


def s5_layer(z, sp, *, uoff, row0, n_seq, seq, h0=None, merged_into=None):
    m_rows, _ = z.shape
    db = sp["d"].shape[1]
    n_cplx = sp["a_re"].shape[1]
    per_block = h0 is not None
    cu = uoff // db
    consts = [sp["b0"], sp["b1"], sp["c0"], sp["c1"], sp["a_re"], sp["a_im"], sp["d"], sp["w_glu"], sp["b_glu"]]
    if per_block:
        assert seq == SUBLANES
        rows = n_seq * seq
        tm = _pick(rows, (256, 128, 64))
        rb0 = row0 // tm
        grid = (rows // tm,)
        in_specs = [pl.BlockSpec((tm, db), lambda i: (rb0 + i, cu))] + [_const_spec(c.shape) for c in consts]
        in_specs.append(pl.BlockSpec((tm // SUBLANES, 2 * n_cplx), lambda i: (i, 0)))
        args = [z] + consts + [h0]
        out_specs = [pl.BlockSpec((tm, db), lambda i: (rb0 + i, 0)),
                     pl.BlockSpec((tm // SUBLANES, 2 * n_cplx), lambda i: (i, 0))]
        st_shape = jax.ShapeDtypeStruct((n_seq, 2 * n_cplx), F32)
        scratch = [pltpu.VMEM((tm, 2 * n_cplx), F32)]
        sem = ("arbitrary",)
    else:
        tm = _pick(seq, (256, 128))
        nt = seq // tm
        rb0 = row0 // tm
        grid = (n_seq, nt)
        in_specs = [pl.BlockSpec((tm, db), lambda n, t: (rb0 + n * nt + t, cu))] + [_const_spec(c.shape) for c in consts]
        args = [z] + consts
        out_specs = [pl.BlockSpec((tm, db), lambda n, t: (rb0 + n * nt + t, 0)),
                     pl.BlockSpec((None, 1, 2 * n_cplx), lambda n, t: (n, 0, 0))]
        st_shape = jax.ShapeDtypeStruct((n_seq, 1, 2 * n_cplx), F32)
        scratch = [pltpu.VMEM((tm, 2 * n_cplx), F32), pltpu.VMEM((1, 2 * n_cplx), F32)]
        sem = ("arbitrary", "arbitrary")
    aliases = {}
    if merged_into is not None:
        in_specs.append(pl.BlockSpec(memory_space=pl.ANY))
        aliases[len(args)] = 0
        args.append(merged_into)
    ob, st = pl.pallas_call(
        functools.partial(_s5_kernel_entry, tm=tm, n_cplx=n_cplx, per_block=per_block,
                          n_alias=len(aliases)),
        grid=grid, in_specs=in_specs, out_specs=out_specs,
        out_shape=[jax.ShapeDtypeStruct((m_rows, db), F32), st_shape],
        scratch_shapes=scratch, input_output_aliases=aliases,
        compiler_params=pltpu.CompilerParams(dimension_semantics=sem, vmem_limit_bytes=VMEM_LIMIT),
        name="s5_sample" if per_block else "s5_prompt",
    )(*args)
    return ob, st.reshape(n_seq, 2 * n_cplx)


def _s5_kernel_entry(*refs, tm, n_cplx, per_block, n_alias):
    n_in = 11 if per_block else 10
    refs = refs[:n_in] + refs[n_in + n_alias:]
    _s5_kernel(*refs, tm=tm, n_cplx=n_cplx, per_block=per_block)


def _cmul(a, b):
    return a[0] * b[0] - a[1] * b[1], a[0] * b[1] + a[1] * b[0]


def s5_params(a_re, a_im, log_dt, b_re, b_im, c_re, c_im, d_skip, w_glu, b_glu):
    g_b, n_s = a_re.shape
    gh = g_b // 2
    hp = lax.Precision.HIGHEST
    lam = lax.complex(a_re.astype(F32), a_im.astype(F32))
    dt = jnp.exp(log_dt.astype(F32))[:, None]
    abar = jnp.exp(lam * dt)
    bbar = ((abar - 1.0) / lam)[..., None] * lax.complex(b_re.astype(F32), b_im.astype(F32))
    eye = jnp.eye(gh, dtype=F32)

    def in_map(half):
        bb = bbar[half * gh:(half + 1) * gh]
        blocks = [jnp.einsum("gsc,gh->gchs", part, eye, precision=hp).reshape(gh * GROUP_CH, gh * n_s)
                  for part in (bb.real, bb.imag)]
        return jnp.concatenate(blocks, axis=1).astype(BF16)

    def out_map(half):
        blocks = [jnp.einsum("gcs,gh->gshc", part[half * gh:(half + 1) * gh], eye,
                             precision=hp).reshape(gh * n_s, gh * GROUP_CH)
                  for part in (c_re.astype(F32), -c_im.astype(F32))]
        return jnp.concatenate(blocks, axis=0).astype(BF16)

    a1 = (abar.real.reshape(1, g_b * n_s), abar.imag.reshape(1, g_b * n_s))
    pw = [a1]
    for _ in range(SUBLANES - 1):
        pw.append(_cmul(pw[-1], a1))
    row = jnp.arange(SUBLANES)[:, None]
    tabs_re, tabs_im = [], []
    for sh in (1, 2, 4):
        tabs_re.append(jnp.where(row >= sh, pw[sh - 1][0], 0.0))
        tabs_im.append(jnp.where(row >= sh, pw[sh - 1][1], 0.0))
    tabs_re.append(jnp.concatenate([p[0] for p in pw], axis=0))
    tabs_im.append(jnp.concatenate([p[1] for p in pw], axis=0))
    return dict(b0=in_map(0), b1=in_map(1), c0=out_map(0), c1=out_map(1),
                a_re=jnp.concatenate(tabs_re, axis=0), a_im=jnp.concatenate(tabs_im, axis=0),
                d=d_skip.astype(F32).reshape(1, -1), w_glu=w_glu.astype(BF16),
                b_glu=b_glu.astype(F32).reshape(1, -1))


def _state_to_lanes(re, im):
    n, g_b, n_s = re.shape
    gh = g_b // 2
    parts = []
    for half in range(2):
        for x in (re, im):
            parts.append(x[:, half * gh:(half + 1) * gh].reshape(n, gh * n_s))
    return jnp.concatenate(parts, axis=1).astype(F32)


def _lanes_to_state(st, g_b, n_s):
    n = st.shape[0]
    gh = g_b // 2
    st = st.reshape(n, 2, 2, gh, n_s)
    re = st[:, :, 0].reshape(n, g_b, n_s)
    im = st[:, :, 1].reshape(n, g_b, n_s)
    return re, im


def _post_kernel(*refs, n_a, ff_chunk):
    x_ref = refs[0]
    a_refs = refs[1:1 + n_a]
    p_ref = refs[1 + n_a]
    wo_refs = refs[2 + n_a:2 + 2 * n_a]
    (g_pm, g_pf, w1_ref, w2_ref, g_qf, g_ple, wg_ref, wp_ref, o_ref) = refs[2 + 2 * n_a:]

    x = x_ref[...]
    mix = None
    for a_ref, w_ref in zip(a_refs, wo_refs):
        t = jnp.dot(a_ref[...].astype(BF16), w_ref[...], preferred_element_type=F32)
        mix = t if mix is None else mix + t
    x = x + _rms(mix, g_pm[...])
    hf = _rms(x, g_pf[...]).astype(BF16)
    d_ff = w1_ref.shape[1]
    f = None
    for c in range(0, d_ff, ff_chunk):
        t = jnp.dot(hf, w1_ref[:, c:c + ff_chunk], preferred_element_type=F32)
        t = jnp.square(jnp.maximum(t, 0.0)).astype(BF16)
        t = jnp.dot(t, w2_ref[c:c + ff_chunk, :], preferred_element_type=F32)
        f = t if f is None else f + t
    x = x + _rms(f, g_qf[...])
    gate = jax.nn.sigmoid(jnp.dot(_rms(x, g_ple[...]).astype(BF16), wg_ref[...], preferred_element_type=F32))
    pe = jnp.dot(p_ref[...].astype(BF16), wp_ref[...], preferred_element_type=F32)
    o_ref[...] = x + pe * gate


def post_block(x, a_list, wo_list, p_all, layer, g_pm, g_pf, w1, w2, g_qf, g_ple, wg, wp):
    m, d = x.shape
    tm = _pick(m, (512, 256, 128))
    n_a = len(a_list)
    consts = list(wo_list) + [g_pm, g_pf, w1, w2, g_qf, g_ple, wg, wp]
    in_specs = [pl.BlockSpec((tm, d), lambda i: (i, 0))]
    in_specs += [pl.BlockSpec((tm, a.shape[1]), lambda i: (i, 0)) for a in a_list]
    in_specs += [pl.BlockSpec((None, tm, p_all.shape[2]), lambda i: (layer, i, 0))]
    in_specs += [pl.BlockSpec(c.shape, lambda i, nd=c.ndim: (0,) * nd, pipeline_mode=pl.Buffered(1)) for c in consts]
    return pl.pallas_call(
        functools.partial(_post_kernel, n_a=n_a, ff_chunk=min(1024, w1.shape[1])),
        grid=(m // tm,),
        in_specs=in_specs,
        out_specs=pl.BlockSpec((tm, d), lambda i: (i, 0)),
        out_shape=jax.ShapeDtypeStruct((m, d), F32),
        compiler_params=pltpu.CompilerParams(dimension_semantics=("parallel",), vmem_limit_bytes=VMEM_LIMIT),
        name="post_block",
    )(x, *a_list, p_all, *consts)


def kernel(x_prompt, x_sample, cache_a1_kv, cache_a2_kv, cache_a3_kv, state_b_re, state_b_im, cache_c_kv, p_prompt, p_sample, w_in_ab, w_out_ab, ssm_a_re, ssm_a_im, ssm_log_dt, ssm_b_re, ssm_b_im, ssm_c_re, ssm_c_im, ssm_d, w_glu, b_glu, w_in_c, sinks_c, w_out_c, g_pre_mix, g_post_mix, g_pre_ffn, g_post_ffn, g_ple, w_ff1, w_ff2, w_ple, w_ple_gate):
    batch, seq, d_model = x_prompt.shape
    ns, n_new, _ = x_sample.shape
    depth = g_pre_mix.shape[0]
    n_pt = batch * seq
    n_st = ns * n_new
    h_a = d_model // (2 * HEAD_DIM)
    d_a = h_a * HEAD_DIM
    n_dil = len(DILATED)
    qkv_a = n_dil * 3 * d_a
    g_b, n_s = ssm_a_re.shape[1:]
    h_c = d_model // HEAD_DIM
    kv_c = cache_c_kv.shape[4]
    rep_c = h_c // kv_c
    wk_c = kv_c * HEAD_DIM

    x = jnp.concatenate([x_prompt.reshape(n_pt, d_model), x_sample.reshape(n_st, d_model)], axis=0)
    p_all = jnp.concatenate([p_prompt.reshape(depth, n_pt, -1), p_sample.reshape(depth, n_st, -1)], axis=1)

    caches_a = [_windows_feature_major(c) for c in (cache_a1_kv, cache_a2_kv, cache_a3_kv)]
    cache_c = _windows_feature_major(cache_c_kv)

    perm = jnp.arange(h_c * HEAD_DIM).reshape(kv_c, rep_c, HEAD_DIM).transpose(1, 0, 2).reshape(-1)

    def row_g(v):
        return v.astype(F32).reshape(1, -1)

    def kv_tail(z, keep, koff, width, heads):
        t = [jnp.stack([z[(b + 1) * seq - keep:(b + 1) * seq, c0:c0 + width] for c0 in (koff, koff + width)])
             for b in range(batch)]
        return jnp.stack(t).reshape(batch, 2, keep, heads, HEAD_DIM)

    new_a = [None] * n_dil
    new_c = None
    a_prompt = [[] for _ in range(n_dil)]
    c_prompt = []
    st_prompt, st_sample = [], []

    for i in range(depth):
        j = i // 2
        if i % 2 == 0:
            z = norm_matmul(x, row_g(g_pre_mix[i]), w_in_ab[j].astype(BF16))
            for g, (win, dil) in enumerate(DILATED):
                a_prompt[g].append(kv_tail(z, min(win, seq), g * 3 * d_a + d_a, d_a, h_a))
            pairs = []
            for g in (2, 1, 0):
                win, dil = DILATED[g]
                off = g * 3 * d_a
                common = dict(dil=dil, qoff=off, koff=off + d_a, voff=off + 2 * d_a, hq=h_a, hkv=h_a)
                if g > 0:
                    on, ls = prompt_attn(z[:, off:off + 3 * d_a], batch=batch, seq=seq, mode="lse", dil=dil,
                                         qoff=0, koff=d_a, voff=2 * d_a, hq=h_a, hkv=h_a)
                    new_a[g], (on, ls) = sample_attn(z, caches_a[g], j, row0=n_pt, n_new=n_new, mode="lse",
                                                     prev_cache=new_a[g], merged_into=(on, ls), **common)
                    pairs += [on, ls]
                else:
                    out_a = prompt_attn(z, batch=batch, seq=seq, mode="merge", extras=pairs, **common)
                    new_a[g], (out_a,) = sample_attn(z, caches_a[g], j, row0=n_pt, n_new=n_new, mode="merge",
                                                     prev_cache=new_a[g], extras=pairs, merged_into=(out_a,),
                                                     **common)
            sp = s5_params(ssm_a_re[j], ssm_a_im[j], ssm_log_dt[j], ssm_b_re[j], ssm_b_im[j], ssm_c_re[j],
                           ssm_c_im[j], ssm_d[j], w_glu[j], b_glu[j])
            out_b, st_p = s5_layer(z, sp, uoff=qkv_a, row0=0, n_seq=batch, seq=seq)
            out_b, st_s = s5_layer(z, sp, uoff=qkv_a, row0=n_pt, n_seq=ns, seq=n_new,
                                   h0=_state_to_lanes(state_b_re[j], state_b_im[j]), merged_into=out_b)
            st_prompt.append(st_p)
            st_sample.append(st_s)
            wo = w_out_ab[j].astype(BF16)
            a_list, wo_list = [out_a, out_b], [wo[:d_a], wo[d_a:]]
        else:
            w_in = w_in_c[j]
            w_in = jnp.concatenate([w_in[:, :h_c * HEAD_DIM][:, perm], w_in[:, h_c * HEAD_DIM:]], axis=1)
            z = norm_matmul(x, row_g(g_pre_mix[i]), w_in.astype(BF16))
            qw = h_c * HEAD_DIM
            c_prompt.append(kv_tail(z, min(WIN_C, seq), qw, wk_c, kv_c))
            sink = jnp.broadcast_to(sinks_c[j].astype(F32).reshape(kv_c, rep_c, 1).transpose(1, 0, 2),
                                    (rep_c, kv_c, HEAD_DIM)).reshape(1, qw)
            common = dict(dil=1, qoff=0, koff=qw, voff=qw + wk_c, hq=h_c, hkv=kv_c, mode="sink", sink=sink)
            att = prompt_attn(z, batch=batch, seq=seq, **common)
            new_c, (att,) = sample_attn(z, cache_c, j, row0=n_pt, n_new=n_new, prev_cache=new_c,
                                        merged_into=(att,), **common)
            a_list, wo_list = [att], [w_out_c[j][perm].astype(BF16)]
        x = post_block(x, a_list, wo_list, p_all, i, row_g(g_post_mix[i]), row_g(g_pre_ffn[i]),
                       w_ff1[i].astype(BF16), w_ff2[i].astype(BF16), row_g(g_post_ffn[i]), row_g(g_ple[i]),
                       w_ple_gate[i].astype(BF16), w_ple[i].astype(BF16))

    y_prompt = x[:n_pt].reshape(batch, seq, d_model)
    y_sample = x[n_pt:].reshape(ns, n_new, d_model)
    outs_a = []
    for g in range(n_dil):
        outs_a.append(jnp.stack(a_prompt[g]))
        outs_a.append(_windows_position_major(new_a[g], h_a))
    sp_re, sp_im = zip(*[_lanes_to_state(s, g_b, n_s) for s in st_prompt])
    ss_re, ss_im = zip(*[_lanes_to_state(s, g_b, n_s) for s in st_sample])
    return (y_prompt, y_sample, *outs_a,
            jnp.stack(sp_re), jnp.stack(ss_re), jnp.stack(sp_im), jnp.stack(ss_im),
            jnp.stack(c_prompt), _windows_position_major(new_c, kv_c))
```

```python
import functools

import jax
import jax.numpy as jnp
from jax import lax
from jax.experimental import pallas as pl
from jax.experimental.pallas import tpu as pltpu

F32 = jnp.float32
BF16 = jnp.bfloat16

HEAD_DIM = 64
BLK = 128
DILATED = ((128, 1), (512, 4), (2048, 16))
GROUP_CH = 16
N_STATE = 64
WIN_C = 128
RMS_EPS = 1e-6
NEG_INF = -1e30
SUBLANES = 8
LANES = 128
VMEM_LIMIT = 48 * 1024 * 1024
WINDOW_VMEM_LIMIT = 56 * 1024 * 1024


def _pick(n, prefs):
    for p in prefs:
        if n % p == 0:
            return p
    raise ValueError(f"no tile in {prefs} divides {n}")


def _rms(x, g):
    ms = jnp.mean(x * x, axis=-1, keepdims=True)
    return x * lax.rsqrt(ms + RMS_EPS) * g


def _const_spec(shape):
    nd = len(shape)
    return pl.BlockSpec(shape, lambda *_: (0,) * nd)


def _norm_matmul_kernel(x_ref, g_ref, w_ref, o_ref, *rest, strided, tiles_per_group):
    og_refs, (h_ref, t_ref) = rest[:-2], rest[-2:]
    j = pl.program_id(1)

    @pl.when(j == 0)
    def _():
        h_ref[...] = _rms(x_ref[...], g_ref[...]).astype(BF16)

    res = jnp.dot(h_ref[...], w_ref[...], preferred_element_type=F32)
    o_ref[...] = res
    tm, tn = res.shape
    for (dil, j0), og_ref in zip(strided, og_refs):
        @pl.when((j >= j0) & (j < j0 + tiles_per_group))
        def _(dil=dil, og_ref=og_ref):
            for lc in range(tn // LANES):
                t_ref[lc] = res[:, lc * LANES:(lc + 1) * LANES]
            for r in range(dil):
                for lc in range(tn // LANES):
                    og_ref[:, r * tn + lc * LANES:r * tn + (lc + 1) * LANES] = \
                        t_ref[lc, pl.ds(r, tm // dil, stride=dil), :]


def norm_matmul(x, g, w, strided=()):
    m, d = x.shape
    n = w.shape[1]
    tm = _pick(m, (1024, 512, 256, 128))
    tn = _pick(n, (512, 256, 128))
    tpg = 3
    groups = tuple((dil, c0 // tn) for dil, c0 in strided)
    out_specs = [pl.BlockSpec((tm, tn), lambda i, j: (i, j))]
    out_shape = [jax.ShapeDtypeStruct((m, n), F32)]
    for dil, j0 in groups:
        out_specs.append(pl.BlockSpec((tm // dil, dil * tn),
                                      lambda i, j, j0=j0: (i, jnp.clip(j - j0, 0, tpg - 1))))
        out_shape.append(jax.ShapeDtypeStruct((m // dil, tpg * dil * tn), F32))
    res = pl.pallas_call(
        functools.partial(_norm_matmul_kernel, strided=groups, tiles_per_group=tpg),
        grid=(m // tm, n // tn),
        in_specs=[pl.BlockSpec((tm, d), lambda i, j: (i, 0)),
                  pl.BlockSpec((1, d), lambda i, j: (0, 0)),
                  pl.BlockSpec((d, tn), lambda i, j: (0, j))],
        out_specs=out_specs, out_shape=out_shape,
        scratch_shapes=[pltpu.VMEM((tm, d), BF16), pltpu.VMEM((tn // LANES, tm, LANES), F32)],
        compiler_params=pltpu.CompilerParams(dimension_semantics=("parallel", "arbitrary"),
                                             vmem_limit_bytes=VMEM_LIMIT),
        name="norm_matmul",
    )(x, g, w)
    return res[0], tuple(res[1:])


def _merge3(o, m, l, on1, ls1, on2, ls2):
    ls0 = m + jnp.log(l)
    mx = jnp.maximum(jnp.maximum(ls0, ls1), ls2)
    w0 = jnp.exp(ls0 - mx)
    w1 = jnp.exp(ls1 - mx)
    w2 = jnp.exp(ls2 - mx)
    return (w0 * (o / l) + w1 * on1 + w2 * on2) / (w0 + w1 + w2)


def _with_sink(o, m, l, sk):
    mx = jnp.maximum(m, sk)
    sc = jnp.exp(m - mx)
    return o * sc / (l * sc + jnp.exp(sk - mx))


def _prompt_attn_kernel(*refs, hq, hkv, mode, extra_dils=()):
    q_ref, kp_ref, kc_ref, vp_ref, vc_ref = refs[:5]
    rest = refs[5:]
    if mode == "sink":
        sink_ref, out_ref = rest
    elif mode == "merge":
        ex_refs, out_ref, e_ref = rest[:4], rest[4], rest[5]
        wq = hq * HEAD_DIM
        for a, (ex_ref, dil) in enumerate(zip(ex_refs, extra_dils)):
            for r in range(dil):
                for lc in range(wq // LANES):
                    e_ref[a, lc, pl.ds(r, BLK // dil, stride=dil), :] = \
                        ex_ref[:, r * wq + lc * LANES:r * wq + (lc + 1) * LANES]
    else:
        on_ref, ls_ref = rest

    b = pl.program_id(2)
    qi = lax.broadcasted_iota(jnp.int32, (BLK, 2 * BLK), 0)
    ki = lax.broadcasted_iota(jnp.int32, (BLK, 2 * BLK), 1)
    valid = (ki >= qi) & (ki <= qi + BLK) & ((ki >= BLK) | (b > 0))

    kv = {}
    for h in range(hq):
        g = h % hkv
        if g not in kv:
            sl = slice(g * HEAD_DIM, (g + 1) * HEAD_DIM)
            k2 = jnp.concatenate([kp_ref[:, sl], kc_ref[:, sl]], axis=0).astype(BF16)
            v2 = jnp.concatenate([vp_ref[:, sl], vc_ref[:, sl]], axis=0).astype(BF16)
            kv[g] = (k2, v2)
        k2, v2 = kv[g]
        hs = slice(h * HEAD_DIM, (h + 1) * HEAD_DIM)
        qh = (q_ref[:, hs] * (HEAD_DIM ** -0.5)).astype(BF16)
        s = lax.dot_general(qh, k2, (((1,), (1,)), ((), ())), preferred_element_type=F32)
        s = jnp.where(valid, s, NEG_INF)
        m = jnp.max(s, axis=-1, keepdims=True)
        p = jnp.exp(s - m)
        l = jnp.sum(p, axis=-1, keepdims=True)
        o = jnp.dot(p.astype(BF16), v2, preferred_element_type=F32)
        if mode == "sink":
            out_ref[:, hs] = _with_sink(o, m, l, sink_ref[:, hs])
        elif mode == "merge":
            lc, hl = divmod(h * HEAD_DIM, LANES)
            ex = [e_ref[a, lc, :, hl:hl + HEAD_DIM] for a in range(4)]
            out_ref[:, hs] = _merge3(o, m, l, *ex)
        else:
            on_ref[:, hs] = o / l
            ls_ref[:, hs] = jnp.broadcast_to(m + jnp.log(l), (BLK, HEAD_DIM))


def prompt_attn(z, *, batch, seq, dil, qoff, koff, voff, hq, hkv, mode, extras=(), sink=None):
    wq, wk = hq * HEAD_DIM, hkv * HEAD_DIM
    nbk = seq // dil // BLK
    if dil == 1:
        m_rows, nz = z.shape
        zv = z
        cq, ck, cv = qoff // wq, koff // wk, voff // wk
        sq = sk = 0
    else:
        assert wq == wk and z.shape[1] == 3 * dil * wq
        m_rows = z.shape[0] * dil
        zv = z
        cq, ck, cv = 0, dil, 2 * dil
        sq = sk = 1

    def row(n, b):
        return n * nbk + b

    in_specs = [
        pl.BlockSpec((BLK, wq), lambda n, r, b: (row(n, b), r * sq + cq)),
        pl.BlockSpec((BLK, wk), lambda n, r, b: (row(n, jnp.maximum(b - 1, 0)), r * sk + ck)),
        pl.BlockSpec((BLK, wk), lambda n, r, b: (row(n, b), r * sk + ck)),
        pl.BlockSpec((BLK, wk), lambda n, r, b: (row(n, jnp.maximum(b - 1, 0)), r * sk + cv)),
        pl.BlockSpec((BLK, wk), lambda n, r, b: (row(n, b), r * sk + cv)),
    ]
    args = [zv, zv, zv, zv, zv]
    out_block = pl.BlockSpec((BLK, wq), lambda n, r, b: (row(n, b), r))
    out_one = jax.ShapeDtypeStruct((m_rows // dil, dil * wq), F32)
    if mode == "sink":
        in_specs.append(pl.BlockSpec((1, wq), lambda n, r, b: (0, 0)))
        args.append(sink)
        out_specs, out_shape = out_block, out_one
    scratch = []
    extra_dils = ()
    if mode == "merge":
        assert dil == 1 and len(extras) == 4
        for e, d_e in extras:
            in_specs.append(pl.BlockSpec((BLK // d_e, d_e * wq), lambda n, r, b: (row(n, b), 0)))
            args.append(e)
        extra_dils = tuple(d_e for _, d_e in extras)
        scratch = [pltpu.VMEM((4, wq // LANES, BLK, LANES), F32)]
        out_specs, out_shape = out_block, out_one
    elif mode == "lse":
        out_specs, out_shape = [out_block, out_block], [out_one, out_one]

    res = pl.pallas_call(
        functools.partial(_prompt_attn_kernel, hq=hq, hkv=hkv, mode=mode, extra_dils=extra_dils),
        grid=(batch, dil, nbk),
        in_specs=in_specs, out_specs=out_specs, out_shape=out_shape, scratch_shapes=scratch,
        compiler_params=pltpu.CompilerParams(dimension_semantics=("parallel", "parallel", "arbitrary"),
                                             vmem_limit_bytes=VMEM_LIMIT),
        name=f"prompt_attn_d{dil}_{mode}",
    )(*args)
    if mode == "lse":
        return res[0], res[1]
    return res


def _sample_attn_kernel(*refs, win, dil, hq, hkv, mode, n_alias, n_new):
    wk = hkv * HEAD_DIM
    q_ref, kn_ref, vn_ref, knt_ref, vnt_ref, cache_ref = refs[:6]
    pos = 6
    if mode == "sink":
        sink_ref = refs[pos]
        pos += 1
    elif mode == "merge":
        on1_ref, ls1_ref, on2_ref, ls2_ref = refs[pos:pos + 4]
        pos += 4
    pos += n_alias
    co_ref = refs[pos]
    pos += 1
    if mode == "lse":
        on_ref, ls_ref = refs[pos:pos + 2]
    else:
        out_ref = refs[pos]

    for c, new_t in ((0, knt_ref), (1, vnt_ref)):
        co_ref[c, :, 0:win - n_new] = cache_ref[c, :, n_new:win]
        co_ref[c, :, win - n_new:win] = new_t[...]

    rows = hq * n_new
    reps = hq // hkv
    q = q_ref[...] * (HEAD_DIM ** -0.5)
    lane_head = lax.broadcasted_iota(jnp.int32, (hkv * n_new, wk), 1) // HEAD_DIM
    row_head = lax.broadcasted_iota(jnp.int32, (hkv * n_new, wk), 0) // n_new
    q_blk = jnp.concatenate(
        [jnp.where(lane_head == row_head, jnp.concatenate([q[:, r * wk:(r + 1) * wk]] * hkv, axis=0), 0.0)
         for r in range(reps)], axis=0).astype(BF16)

    s_c = jnp.dot(q_blk, cache_ref[0].astype(BF16), preferred_element_type=F32)
    s_n = lax.dot_general(q_blk, kn_ref[...].astype(BF16), (((1,), (1,)), ((), ())),
                          preferred_element_type=F32)
    tok_c = lax.broadcasted_iota(jnp.int32, (rows, win), 0) % n_new
    pos_c = lax.broadcasted_iota(jnp.int32, (rows, win), 1)
    ok_c = (pos_c >= tok_c) & (((win + tok_c - pos_c) & (dil - 1)) == 0)
    tok_n = lax.broadcasted_iota(jnp.int32, (rows, n_new), 0) % n_new
    pos_n = lax.broadcasted_iota(jnp.int32, (rows, n_new), 1)
    ok_n = (pos_n <= tok_n) & (((tok_n - pos_n) & (dil - 1)) == 0)
    s_c = jnp.where(ok_c, s_c, NEG_INF)
    s_n = jnp.where(ok_n, s_n, NEG_INF)
    m = jnp.maximum(jnp.max(s_c, axis=-1, keepdims=True), jnp.max(s_n, axis=-1, keepdims=True))
    p_c = jnp.exp(s_c - m)
    p_n = jnp.exp(s_n - m)
    l = jnp.sum(p_c, axis=-1, keepdims=True) + jnp.sum(p_n, axis=-1, keepdims=True)
    o = lax.dot_general(p_c.astype(BF16), cache_ref[1].astype(BF16), (((1,), (1,)), ((), ())),
                        preferred_element_type=F32)
    o = o + jnp.dot(p_n.astype(BF16), vn_ref[...].astype(BF16), preferred_element_type=F32)

    for h in range(hq):
        g = h % hkv
        rs = slice(h * n_new, (h + 1) * n_new)
        hs = slice(h * HEAD_DIM, (h + 1) * HEAD_DIM)
        o_h, m_h, l_h = o[rs, g * HEAD_DIM:(g + 1) * HEAD_DIM], m[rs], l[rs]
        if mode == "sink":
            out_ref[:, hs] = _with_sink(o_h, m_h, l_h, sink_ref[:, hs])
        elif mode == "merge":
            out_ref[:, hs] = _merge3(o_h, m_h, l_h, on1_ref[:, hs], ls1_ref[:, hs], on2_ref[:, hs], ls2_ref[:, hs])
        else:
            on_ref[:, hs] = o_h / l_h
            ls_ref[:, hs] = jnp.broadcast_to(m_h + jnp.log(l_h), (n_new, HEAD_DIM))


def sample_attn(z, cache_t, layer, *, row0, n_new, dil, qoff, koff, voff, hq, hkv, mode,
                prev_cache=None, extras=(), sink=None, merged_into=()):
    m_rows, nz = z.shape
    n_layers, ns, _, wk, win = cache_t.shape
    wq = hq * HEAD_DIM
    assert win == dil * BLK and row0 % n_new == 0 and dil & (dil - 1) == 0
    rb0 = row0 // n_new
    cq, ck, cv = qoff // wq, koff // wk, voff // wk
    znew = z[row0:row0 + ns * n_new]
    knt = znew[:, koff:koff + wk].reshape(ns, n_new, wk).transpose(0, 2, 1)
    vnt = znew[:, voff:voff + wk].reshape(ns, n_new, wk).transpose(0, 2, 1)

    new_t = pl.BlockSpec((None, wk, n_new), lambda i: (i, 0, 0))
    win_block = pl.BlockSpec((None, None, 2, wk, win), lambda i: (layer, i, 0, 0, 0))
    in_specs = [
        pl.BlockSpec((n_new, wq), lambda i: (rb0 + i, cq)),
        pl.BlockSpec((n_new, wk), lambda i: (rb0 + i, ck)),
        pl.BlockSpec((n_new, wk), lambda i: (rb0 + i, cv)),
        new_t, new_t, win_block,
    ]
    args = [z, z, z, knt, vnt, cache_t]
    if mode == "sink":
        in_specs.append(_const_spec(sink.shape))
        args.append(sink)
    elif mode == "merge":
        for e in extras:
            in_specs.append(pl.BlockSpec((n_new, wq), lambda i: (i, 0)))
            args.append(e)
    aliases = {}
    alias_args = ([prev_cache] if prev_cache is not None else []) + list(merged_into)
    first_out = 0 if prev_cache is not None else 1
    for a_i, a in enumerate(alias_args):
        in_specs.append(pl.BlockSpec(memory_space=pl.ANY))
        aliases[len(args)] = first_out + a_i
        args.append(a)

    n_tok_out = 2 if mode == "lse" else 1
    if merged_into:
        assert len(merged_into) == n_tok_out
        tok_block = pl.BlockSpec((n_new, wq), lambda i: (rb0 + i, 0))
        tok_shape = jax.ShapeDtypeStruct((m_rows, wq), F32)
    else:
        tok_block = pl.BlockSpec((n_new, wq), lambda i: (i, 0))
        tok_shape = jax.ShapeDtypeStruct((ns * n_new, wq), F32)
    res = pl.pallas_call(
        functools.partial(_sample_attn_kernel, win=win, dil=dil, hq=hq, hkv=hkv, mode=mode,
                          n_alias=len(alias_args), n_new=n_new),
        grid=(ns,),
        in_specs=in_specs,
        out_specs=[win_block] + [tok_block] * n_tok_out,
        out_shape=[jax.ShapeDtypeStruct(cache_t.shape, F32)] + [tok_shape] * n_tok_out,
        input_output_aliases=aliases,
        compiler_params=pltpu.CompilerParams(dimension_semantics=("parallel",), vmem_limit_bytes=WINDOW_VMEM_LIMIT),
        name=f"sample_attn_d{dil}_{mode}",
    )(*args)
    return res[0], tuple(res[1:])


def _windows_feature_major(cache):
    n_l, ns, _, win, h, hd = cache.shape
    return cache.transpose(0, 1, 2, 4, 5, 3).reshape(n_l, ns, 2, h * hd, win)


def _windows_position_major(cache_t, heads):
    n_l, ns, _, wk, win = cache_t.shape
    return cache_t.reshape(n_l, ns, 2, heads, wk // heads, win).transpose(0, 1, 2, 5, 3, 4)


def _s5_kernel(*refs, tm, n_cplx, per_block):
    (u_ref, b0_ref, b1_ref, c0_ref, c1_ref, are_ref, aim_ref, d_ref, wglu_ref, bglu_ref) = refs[:10]
    if per_block:
        h0_ref, ob_ref, st_ref, hs_ref = refs[10:]
    else:
        ob_ref, st_ref, hs_ref, carry_ref = refs[10:]

        @pl.when(pl.program_id(1) == 0)
        def _():
            carry_ref[...] = jnp.zeros_like(carry_ref)

    half = n_cplx // 2
    u = u_ref[...]
    ub = u.astype(BF16)
    dh = u.shape[1] // 2
    hs_ref[:, 0:2 * half] = jnp.dot(ub[:, :dh], b0_ref[...], preferred_element_type=F32)
    hs_ref[:, 2 * half:4 * half] = jnp.dot(ub[:, dh:], b1_ref[...], preferred_element_type=F32)

    n_blk = tm // SUBLANES
    cols_per_loop = 2
    for c0 in range(0, n_cplx // LANES, cols_per_loop):
        consts, lanes = [], []
        for c in range(c0, c0 + cols_per_loop):
            sl = slice(c * LANES, (c + 1) * LANES)
            ar, ai = are_ref[:, sl], aim_ref[:, sl]
            consts.append([(ar[k * 8:(k + 1) * 8], ai[k * 8:(k + 1) * 8]) for k in range(4)])
            base = (c * LANES // half) * 2 * half + (c * LANES) % half
            lanes.append((slice(base, base + LANES), slice(base + half, base + half + LANES)))

        def scan8(k, ci, cr, cim, consts=consts, lanes=lanes):
            rows = pl.ds(pl.multiple_of(k * SUBLANES, SUBLANES), SUBLANES)
            lr, li = lanes[ci]
            xr, xi = hs_ref[rows, lr], hs_ref[rows, li]
            for step, sh in enumerate((1, 2, 4)):
                ar, ai = consts[ci][step]
                sr, si = pltpu.roll(xr, sh, 0), pltpu.roll(xi, sh, 0)
                xr, xi = xr + ar * sr - ai * si, xi + ar * si + ai * sr
            pr, pi = consts[ci][3]
            xr, xi = xr + pr * cr - pi * cim, xi + pr * cim + pi * cr
            hs_ref[rows, lr] = xr
            hs_ref[rows, li] = xi
            return xr[SUBLANES - 1:SUBLANES], xi[SUBLANES - 1:SUBLANES]

        def blk(k, carry):
            return tuple(scan8(k, ci, *carry[ci]) for ci in range(cols_per_loop))

        def blk_own_state(kk, carry, lanes=lanes):
            grp = pl.ds(pl.multiple_of(kk * SUBLANES, SUBLANES), SUBLANES)
            for ci in range(cols_per_loop):
                h0r, h0i = h0_ref[grp, lanes[ci][0]], h0_ref[grp, lanes[ci][1]]
                last = [scan8(kk * SUBLANES + jj, ci, h0r[jj:jj + 1], h0i[jj:jj + 1]) for jj in range(SUBLANES)]
                st_ref[grp, lanes[ci][0]] = jnp.concatenate([t[0] for t in last], axis=0)
                st_ref[grp, lanes[ci][1]] = jnp.concatenate([t[1] for t in last], axis=0)
            return carry

        if per_block:
            lax.fori_loop(0, n_blk // SUBLANES, blk_own_state, 0)
        else:
            init = tuple((carry_ref[:, lanes[ci][0]], carry_ref[:, lanes[ci][1]]) for ci in range(cols_per_loop))
            fin = lax.fori_loop(0, n_blk, blk, init, unroll=2)
            for ci in range(cols_per_loop):
                carry_ref[:, lanes[ci][0]] = fin[ci][0]
                carry_ref[:, lanes[ci][1]] = fin[ci][1]

    if not per_block:
        st_ref[...] = carry_ref[...]

    y_lo = jnp.dot(hs_ref[:, 0:2 * half].astype(BF16), c0_ref[...], preferred_element_type=F32)
    y_hi = jnp.dot(hs_ref[:, 2 * half:4 * half].astype(BF16), c1_ref[...], preferred_element_type=F32)
    y = jnp.concatenate([y_lo, y_hi], axis=1) + d_ref[...] * u
    zb = jax.nn.gelu(y)
    gate = jnp.dot(zb.astype(BF16), wglu_ref[...], preferred_element_type=F32) + bglu_ref[...]
    ob_ref[...] = zb * jax.nn.sigmoid(gate)


def s5_layer(z, sp, *, uoff, row0, n_seq, seq, h0=None, merged_into=None):
    m_rows, _ = z.shape
    db = sp["d"].shape[1]
    n_cplx = sp["a_re"].shape[1]
    per_block = h0 is not None
    cu = uoff // db
    consts = [sp["b0"], sp["b1"], sp["c0"], sp["c1"], sp["a_re"], sp["a_im"], sp["d"], sp["w_glu"], sp["b_glu"]]
    if per_block:
        assert seq == SUBLANES
        rows = n_seq * seq
        tm = _pick(rows, (256, 128, 64))
        rb0 = row0 // tm
        grid = (rows // tm,)
        in_specs = [pl.BlockSpec((tm, db), lambda i: (rb0 + i, cu))] + [_const_spec(c.shape) for c in consts]
        in_specs.append(pl.BlockSpec((tm // SUBLANES, 2 * n_cplx), lambda i: (i, 0)))
        args = [z] + consts + [h0]
        out_specs = [pl.BlockSpec((tm, db), lambda i: (rb0 + i, 0)),
                     pl.BlockSpec((tm // SUBLANES, 2 * n_cplx), lambda i: (i, 0))]
        st_shape = jax.ShapeDtypeStruct((n_seq, 2 * n_cplx), F32)
        scratch = [pltpu.VMEM((tm, 2 * n_cplx), F32)]
        sem = ("arbitrary",)
    else:
        tm = _pick(seq, (256, 128))
        nt = seq // tm
        rb0 = row0 // tm
        grid = (n_seq, nt)
        in_specs = [pl.BlockSpec((tm, db), lambda n, t: (rb0 + n * nt + t, cu))] + [_const_spec(c.shape) for c in consts]
        args = [z] + consts
        out_specs = [pl.BlockSpec((tm, db), lambda n, t: (rb0 + n * nt + t, 0)),
                     pl.BlockSpec((None, 1, 2 * n_cplx), lambda n, t: (n, 0, 0))]
        st_shape = jax.ShapeDtypeStruct((n_seq, 1, 2 * n_cplx), F32)
        scratch = [pltpu.VMEM((tm, 2 * n_cplx), F32), pltpu.VMEM((1, 2 * n_cplx), F32)]
        sem = ("arbitrary", "arbitrary")
    aliases = {}
    if merged_into is not None:
        in_specs.append(pl.BlockSpec(memory_space=pl.ANY))
        aliases[len(args)] = 0
        args.append(merged_into)
    ob, st = pl.pallas_call(
        functools.partial(_s5_kernel_entry, tm=tm, n_cplx=n_cplx, per_block=per_block,
                          n_alias=len(aliases)),
        grid=grid, in_specs=in_specs, out_specs=out_specs,
        out_shape=[jax.ShapeDtypeStruct((m_rows, db), F32), st_shape],
        scratch_shapes=scratch, input_output_aliases=aliases,
        compiler_params=pltpu.CompilerParams(dimension_semantics=sem, vmem_limit_bytes=VMEM_LIMIT),
        name="s5_sample" if per_block else "s5_prompt",
    )(*args)
    return ob, st.reshape(n_seq, 2 * n_cplx)


def _s5_kernel_entry(*refs, tm, n_cplx, per_block, n_alias):
    n_in = 11 if per_block else 10
    refs = refs[:n_in] + refs[n_in + n_alias:]
    _s5_kernel(*refs, tm=tm, n_cplx=n_cplx, per_block=per_block)


def _cmul(a, b):
    return a[0] * b[0] - a[1] * b[1], a[0] * b[1] + a[1] * b[0]


def s5_params(a_re, a_im, log_dt, b_re, b_im, c_re, c_im, d_skip, w_glu, b_glu):
    g_b, n_s = a_re.shape
    gh = g_b // 2
    hp = lax.Precision.HIGHEST
    lam = lax.complex(a_re.astype(F32), a_im.astype(F32))
    dt = jnp.exp(log_dt.astype(F32))[:, None]
    abar = jnp.exp(lam * dt)
    bbar = ((abar - 1.0) / lam)[..., None] * lax.complex(b_re.astype(F32), b_im.astype(F32))
    eye = jnp.eye(gh, dtype=F32)

    def in_map(half):
        bb = bbar[half * gh:(half + 1) * gh]
        blocks = [jnp.einsum("gsc,gh->gchs", part, eye, precision=hp).reshape(gh * GROUP_CH, gh * n_s)
                  for part in (bb.real, bb.imag)]
        return jnp.concatenate(blocks, axis=1).astype(BF16)

    def out_map(half):
        blocks = [jnp.einsum("gcs,gh->gshc", part[half * gh:(half + 1) * gh], eye,
                             precision=hp).reshape(gh * n_s, gh * GROUP_CH)
                  for part in (c_re.astype(F32), -c_im.astype(F32))]
        return jnp.concatenate(blocks, axis=0).astype(BF16)

    a1 = (abar.real.reshape(1, g_b * n_s), abar.imag.reshape(1, g_b * n_s))
    pw = [a1]
    for _ in range(SUBLANES - 1):
        pw.append(_cmul(pw[-1], a1))
    row = jnp.arange(SUBLANES)[:, None]
    tabs_re, tabs_im = [], []
    for sh in (1, 2, 4):
        tabs_re.append(jnp.where(row >= sh, pw[sh - 1][0], 0.0))
        tabs_im.append(jnp.where(row >= sh, pw[sh - 1][1], 0.0))
    tabs_re.append(jnp.concatenate([p[0] for p in pw], axis=0))
    tabs_im.append(jnp.concatenate([p[1] for p in pw], axis=0))
    return dict(b0=in_map(0), b1=in_map(1), c0=out_map(0), c1=out_map(1),
                a_re=jnp.concatenate(tabs_re, axis=0), a_im=jnp.concatenate(tabs_im, axis=0),
                d=d_skip.astype(F32).reshape(1, -1), w_glu=w_glu.astype(BF16),
                b_glu=b_glu.astype(F32).reshape(1, -1))


def _state_to_lanes(re, im):
    n, g_b, n_s = re.shape
    gh = g_b // 2
    parts = []
    for half in range(2):
        for x in (re, im):
            parts.append(x[:, half * gh:(half + 1) * gh].reshape(n, gh * n_s))
    return jnp.concatenate(parts, axis=1).astype(F32)


def _lanes_to_state(st, g_b, n_s):
    n = st.shape[0]
    gh = g_b // 2
    st = st.reshape(n, 2, 2, gh, n_s)
    re = st[:, :, 0].reshape(n, g_b, n_s)
    im = st[:, :, 1].reshape(n, g_b, n_s)
    return re, im


def _post_kernel(*refs, n_a, ff_chunk):
    x_ref = refs[0]
    a_refs = refs[1:1 + n_a]
    p_ref = refs[1 + n_a]
    wo_refs = refs[2 + n_a:2 + 2 * n_a]
    (g_pm, g_pf, w1_ref, w2_ref, g_qf, g_ple, wg_ref, wp_ref, o_ref) = refs[2 + 2 * n_a:]

    x = x_ref[...]
    mix = None
    for a_ref, w_ref in zip(a_refs, wo_refs):
        t = jnp.dot(a_ref[...].astype(BF16), w_ref[...], preferred_element_type=F32)
        mix = t if mix is None else mix + t
    x = x + _rms(mix, g_pm[...])
    hf = _rms(x, g_pf[...]).astype(BF16)
    d_ff = w1_ref.shape[1]
    f = None
    for c in range(0, d_ff, ff_chunk):
        t = jnp.dot(hf, w1_ref[:, c:c + ff_chunk], preferred_element_type=F32)
        t = jnp.square(jnp.maximum(t, 0.0)).astype(BF16)
        t = jnp.dot(t, w2_ref[c:c + ff_chunk, :], preferred_element_type=F32)
        f = t if f is None else f + t
    x = x + _rms(f, g_qf[...])
    gate = jax.nn.sigmoid(jnp.dot(_rms(x, g_ple[...]).astype(BF16), wg_ref[...], preferred_element_type=F32))
    pe = jnp.dot(p_ref[...].astype(BF16), wp_ref[...], preferred_element_type=F32)
    o_ref[...] = x + pe * gate


def post_block(x, a_list, wo_list, p_all, layer, g_pm, g_pf, w1, w2, g_qf, g_ple, wg, wp):
    m, d = x.shape
    tm = _pick(m, (512, 256, 128))
    n_a = len(a_list)
    consts = list(wo_list) + [g_pm, g_pf, w1, w2, g_qf, g_ple, wg, wp]
    in_specs = [pl.BlockSpec((tm, d), lambda i: (i, 0))]
    in_specs += [pl.BlockSpec((tm, a.shape[1]), lambda i: (i, 0)) for a in a_list]
    in_specs += [pl.BlockSpec((None, tm, p_all.shape[2]), lambda i: (layer, i, 0))]
    in_specs += [pl.BlockSpec(c.shape, lambda i, nd=c.ndim: (0,) * nd, pipeline_mode=pl.Buffered(1)) for c in consts]
    return pl.pallas_call(
        functools.partial(_post_kernel, n_a=n_a, ff_chunk=min(1024, w1.shape[1])),
        grid=(m // tm,),
        in_specs=in_specs,
        out_specs=pl.BlockSpec((tm, d), lambda i: (i, 0)),
        out_shape=jax.ShapeDtypeStruct((m, d), F32),
        compiler_params=pltpu.CompilerParams(dimension_semantics=("parallel",), vmem_limit_bytes=VMEM_LIMIT),
        name="post_block",
    )(x, *a_list, p_all, *consts)


def kernel(x_prompt, x_sample, cache_a1_kv, cache_a2_kv, cache_a3_kv, state_b_re, state_b_im, cache_c_kv, p_prompt, p_sample, w_in_ab, w_out_ab, ssm_a_re, ssm_a_im, ssm_log_dt, ssm_b_re, ssm_b_im, ssm_c_re, ssm_c_im, ssm_d, w_glu, b_glu, w_in_c, sinks_c, w_out_c, g_pre_mix, g_post_mix, g_pre_ffn, g_post_ffn, g_ple, w_ff1, w_ff2, w_ple, w_ple_gate):
    batch, seq, d_model = x_prompt.shape
    ns, n_new, _ = x_sample.shape
    depth = g_pre_mix.shape[0]
    n_pt = batch * seq
    n_st = ns * n_new
    h_a = d_model // (2 * HEAD_DIM)
    d_a = h_a * HEAD_DIM
    n_dil = len(DILATED)
    qkv_a = n_dil * 3 * d_a
    g_b, n_s = ssm_a_re.shape[1:]
    h_c = d_model // HEAD_DIM
    kv_c = cache_c_kv.shape[4]
    rep_c = h_c // kv_c
    wk_c = kv_c * HEAD_DIM

    x = jnp.concatenate([x_prompt.reshape(n_pt, d_model), x_sample.reshape(n_st, d_model)], axis=0)
    p_all = jnp.concatenate([p_prompt.reshape(depth, n_pt, -1), p_sample.reshape(depth, n_st, -1)], axis=1)

    caches_a = [_windows_feature_major(c) for c in (cache_a1_kv, cache_a2_kv, cache_a3_kv)]
    cache_c = _windows_feature_major(cache_c_kv)

    perm = jnp.arange(h_c * HEAD_DIM).reshape(kv_c, rep_c, HEAD_DIM).transpose(1, 0, 2).reshape(-1)

    def row_g(v):
        return v.astype(F32).reshape(1, -1)

    def kv_tail(z, keep, koff, width, heads):
        t = [jnp.stack([z[(b + 1) * seq - keep:(b + 1) * seq, c0:c0 + width] for c0 in (koff, koff + width)])
             for b in range(batch)]
        return jnp.stack(t).reshape(batch, 2, keep, heads, HEAD_DIM)

    new_a = [None] * n_dil
    new_c = None
    a_prompt = [[] for _ in range(n_dil)]
    c_prompt = []
    st_prompt, st_sample = [], []

    for i in range(depth):
        j = i // 2
        if i % 2 == 0:
            dilated = [g for g in range(n_dil) if DILATED[g][1] > 1]
            z, z_strided = norm_matmul(x, row_g(g_pre_mix[i]), w_in_ab[j].astype(BF16),
                                       strided=[(DILATED[g][1], g * 3 * d_a) for g in dilated])
            z_strided = dict(zip(dilated, z_strided))
            for g, (win, dil) in enumerate(DILATED):
                a_prompt[g].append(kv_tail(z, min(win, seq), g * 3 * d_a + d_a, d_a, h_a))
            pairs_p, pairs_s = [], []
            for g in (2, 1, 0):
                win, dil = DILATED[g]
                off = g * 3 * d_a
                common = dict(dil=dil, qoff=off, koff=off + d_a, voff=off + 2 * d_a, hq=h_a, hkv=h_a)
                if g > 0:
                    on, ls = prompt_attn(z_strided[g], batch=batch, seq=seq, mode="lse", dil=dil,
                                         qoff=0, koff=d_a, voff=2 * d_a, hq=h_a, hkv=h_a)
                    pairs_p += [(on, dil), (ls, dil)]
                    new_a[g], on_ls = sample_attn(z, caches_a[g], j, row0=n_pt, n_new=n_new, mode="lse",
                                                  prev_cache=new_a[g], **common)
                    pairs_s += on_ls
                else:
                    out_a = prompt_attn(z, batch=batch, seq=seq, mode="merge", extras=pairs_p, **common)
                    new_a[g], (out_a,) = sample_attn(z, caches_a[g], j, row0=n_pt, n_new=n_new, mode="merge",
                                                     prev_cache=new_a[g], extras=pairs_s, merged_into=(out_a,),
                                                     **common)
            sp = s5_params(ssm_a_re[j], ssm_a_im[j], ssm_log_dt[j], ssm_b_re[j], ssm_b_im[j], ssm_c_re[j],
                           ssm_c_im[j], ssm_d[j], w_glu[j], b_glu[j])
            out_b, st_p = s5_layer(z, sp, uoff=qkv_a, row0=0, n_seq=batch, seq=seq)
            out_b, st_s = s5_layer(z, sp, uoff=qkv_a, row0=n_pt, n_seq=ns, seq=n_new,
                                   h0=_state_to_lanes(state_b_re[j], state_b_im[j]), merged_into=out_b)
            st_prompt.append(st_p)
            st_sample.append(st_s)
            wo = w_out_ab[j].astype(BF16)
            a_list, wo_list = [out_a, out_b], [wo[:d_a], wo[d_a:]]
        else:
            w_in = w_in_c[j]
            w_in = jnp.concatenate([w_in[:, :h_c * HEAD_DIM][:, perm], w_in[:, h_c * HEAD_DIM:]], axis=1)
            z, _ = norm_matmul(x, row_g(g_pre_mix[i]), w_in.astype(BF16))
            qw = h_c * HEAD_DIM
            c_prompt.append(kv_tail(z, min(WIN_C, seq), qw, wk_c, kv_c))
            sink = jnp.broadcast_to(sinks_c[j].astype(F32).reshape(kv_c, rep_c, 1).transpose(1, 0, 2),
                                    (rep_c, kv_c, HEAD_DIM)).reshape(1, qw)
            common = dict(dil=1, qoff=0, koff=qw, voff=qw + wk_c, hq=h_c, hkv=kv_c, mode="sink", sink=sink)
            att = prompt_attn(z, batch=batch, seq=seq, **common)
            new_c, (att,) = sample_attn(z, cache_c, j, row0=n_pt, n_new=n_new, prev_cache=new_c,
                                        merged_into=(att,), **common)
            a_list, wo_list = [att], [w_out_c[j][perm].astype(BF16)]
        x = post_block(x, a_list, wo_list, p_all, i, row_g(g_post_mix[i]), row_g(g_pre_ffn[i]),
                       w_ff1[i].astype(BF16), w_ff2[i].astype(BF16), row_g(g_post_ffn[i]), row_g(g_ple[i]),
                       w_ple_gate[i].astype(BF16), w_ple[i].astype(BF16))

    y_prompt = x[:n_pt].reshape(batch, seq, d_model)
    y_sample = x[n_pt:].reshape(ns, n_new, d_model)
    outs_a = []
    for g in range(n_dil):
        outs_a.append(jnp.stack(a_prompt[g]))
        outs_a.append(_windows_position_major(new_a[g], h_a))
    sp_re, sp_im = zip(*[_lanes_to_state(s, g_b, n_s) for s in st_prompt])
    ss_re, ss_im = zip(*[_lanes_to_state(s, g_b, n_s) for s in st_sample])
    return (y_prompt, y_sample, *outs_a,
            jnp.stack(sp_re), jnp.stack(ss_re), jnp.stack(sp_im), jnp.stack(ss_im),
            jnp.stack(c_prompt), _windows_position_major(new_c, kv_c))
```
